```python
import jax, jax.numpy as jnp
from jax import lax
import numpy as np

D_MODEL = 1024
BATCH = 4
SEQ = 8192
DEPTH = 2

N_MIXERS = 2
MLSTM_HEADS = 4
MLSTM_DK = 128
MLSTM_DV = D_MODEL // MLSTM_HEADS
MLSTM_CHUNK = 128
QK_WIDTH = MLSTM_HEADS * MLSTM_DK
V_WIDTH = MLSTM_HEADS * MLSTM_DV
N_GATE_COLS = 4 * MLSTM_HEADS
MLSTM_IN_WIDTH = 2 * QK_WIDTH + 2 * V_WIDTH + N_GATE_COLS
FORGET_BIAS = 3.0
POOL_WINDOWS = (2, 4, 8, 16)
POOL_GROUPS = len(POOL_WINDOWS)
POOL_GROUP_WIDTH = D_MODEL // POOL_GROUPS
D_FF = 4 * D_MODEL
PLE_DIM = 256
N_MLSTM_LAYERS = (DEPTH + N_MIXERS - 1) // N_MIXERS
N_POOL_LAYERS = DEPTH // N_MIXERS
EPS = 1e-6

kernel_name = 'hybrid_mlstm_pool_encoder'


def rmsnorm(x, gain):
    x32 = x.astype(jnp.float32)
    y = x32 * lax.rsqrt(jnp.mean(jnp.square(x32), axis=-1, keepdims=True) + EPS)
    return (y * gain.astype(jnp.float32)).astype(x.dtype)


def mlstm_scan(q, k, v, i_pre, f_pre):
    B, H, S, DK = q.shape
    DV = v.shape[-1]
    L = MLSTM_CHUNK
    NC = S // L

    def chunks(t):
        return jnp.moveaxis(t.reshape((B, H, NC, L) + t.shape[3:]), 2, 0)

    qc, kc, vc = chunks(q), chunks(k), chunks(v)
    ic = chunks(i_pre)
    bc = jnp.cumsum(chunks(jax.nn.log_sigmoid(f_pre)), axis=-1)
    lower = jnp.tril(jnp.ones((L, L), dtype=bool))

    def step(carry, xs):
        C, n, m = carry
        q_, k_, v_, i_, b_ = xs
        b_last = b_[..., -1]
        log_d = jnp.where(lower, b_[..., :, None] - b_[..., None, :] + i_[..., None, :], -jnp.inf)
        log_inter = b_ + m[..., None]
        m_t = jnp.maximum(log_inter, jnp.max(log_d, axis=-1))
        d = jnp.exp(log_d - m_t[..., None])
        inter = jnp.exp(log_inter - m_t)
        s = jnp.einsum('bhtd,bhsd->bhts', q_, k_) * d
        num = jnp.einsum('bhts,bhsv->bhtv', s, v_) + inter[..., None] * jnp.einsum('bhtd,bhvd->bhtv', q_, C)
        den = jnp.sum(s, axis=-1) + inter * jnp.einsum('bhtd,bhd->bht', q_, n)
        h = num / jnp.maximum(jnp.abs(den), jnp.exp(-m_t))[..., None]
        log_w = b_last[..., None] - b_ + i_
        m_new = jnp.maximum(b_last + m, jnp.max(log_w, axis=-1))
        w = jnp.exp(log_w - m_new[..., None])
        decay = jnp.exp(b_last + m - m_new)
        C = decay[..., None, None] * C + jnp.einsum('bhsv,bhsd->bhvd', w[..., None] * v_, k_)
        n = decay[..., None] * n + jnp.einsum('bhs,bhsd->bhd', w, k_)
        return (C, n, m_new), h

    init = (jnp.zeros((B, H, DV, DK), jnp.float32),
            jnp.zeros((B, H, DK), jnp.float32),
            jnp.zeros((B, H), jnp.float32))
    _, hs = lax.scan(step, init, (qc, kc, vc, ic, bc))
    return jnp.moveaxis(hs, 0, 2).reshape(B, H, S, DV)


def mlstm_mixer(xn, w_in, b_gates, head_norm, w_out):
    B, S, _ = xn.shape
    u = xn @ w_in
    q, k, v, o, g = jnp.split(u, [QK_WIDTH, 2 * QK_WIDTH, 2 * QK_WIDTH + V_WIDTH,
                                  2 * QK_WIDTH + 2 * V_WIDTH], axis=-1)
    q = q.reshape(B, S, MLSTM_HEADS, MLSTM_DK).transpose(0, 2, 1, 3).astype(jnp.float32)
    k = (k.reshape(B, S, MLSTM_HEADS, MLSTM_DK).transpose(0, 2, 1, 3).astype(jnp.float32)
         * (MLSTM_DK ** -0.5))
    v = v.reshape(B, S, MLSTM_HEADS, MLSTM_DV).transpose(0, 2, 1, 3).astype(jnp.float32)
    g = g.reshape(B, S, 4, MLSTM_HEADS).astype(jnp.float32) + b_gates.astype(jnp.float32)
    g = g.transpose(2, 0, 3, 1)
    h_fwd = mlstm_scan(q, k, v, g[0], g[1])
    flip = lambda t: jnp.flip(t, axis=2)
    h_bwd = flip(mlstm_scan(flip(q), flip(k), flip(v), flip(g[2]), flip(g[3])))
    h = (h_fwd + h_bwd).transpose(0, 2, 1, 3)
    h = h * lax.rsqrt(jnp.mean(jnp.square(h), axis=-1, keepdims=True) + EPS)
    h = h * head_norm.astype(jnp.float32).reshape(MLSTM_HEADS, MLSTM_DV)
    h = h.reshape(B, S, V_WIDTH) * jax.nn.sigmoid(o.astype(jnp.float32))
    return h.astype(xn.dtype) @ w_out


def pool_mixer(xn, w_in, w_grp, scale, w_out):
    B, S, D = xn.shape
    u = (xn @ w_in).astype(jnp.float32)
    cs = jnp.concatenate([jnp.zeros((B, 1, D), jnp.float32), jnp.cumsum(u, axis=1)], axis=1)
    t = jnp.arange(S)
    pooled = []
    for gi, win in enumerate(POOL_WINDOWS):
        c0, c1 = gi * POOL_GROUP_WIDTH, (gi + 1) * POOL_GROUP_WIDTH
        lo = jnp.clip(t - win // 2, 0, S)
        hi = jnp.clip(t + win - win // 2, 0, S)
        csg = cs[:, :, c0:c1]
        mean = (csg[:, hi] - csg[:, lo]) / (hi - lo).astype(jnp.float32)[None, :, None]
        pooled.append(mean)
    y = jnp.concatenate(pooled, axis=-1) - u
    y = jnp.einsum('bsgc,gcd->bsgd', y.reshape(B, S, POOL_GROUPS, POOL_GROUP_WIDTH),
                   w_grp.astype(jnp.float32)).reshape(B, S, D)
    y = y * scale.astype(jnp.float32)
    return y.astype(xn.dtype) @ w_out


def sqrelu_mlp(xn, w1, w2):
    return jnp.square(jax.nn.relu(xn @ w1)) @ w2


def setup_inputs(seed: int = 0) -> dict:
    key = jax.random.key(seed)
    ks = jax.random.split(key, 24)

    def nrm(k, shape, fan_in):
        return jax.random.normal(k, shape, jnp.float32) * (fan_in ** -0.5)

    def gain(k, shape):
        return 1.0 + 0.05 * jax.random.normal(k, shape, jnp.float32)

    nA, nB = N_MLSTM_LAYERS, N_POOL_LAYERS
    gate_base = jnp.array([0.0, FORGET_BIAS, 0.0, FORGET_BIAS], jnp.float32)[None, :, None]
    return {
        'x': jax.random.normal(ks[0], (BATCH, SEQ, D_MODEL), jnp.float32),
        'p': jax.random.normal(ks[1], (DEPTH, BATCH, SEQ, PLE_DIM), jnp.float32),
        'norm_mix': gain(ks[2], (DEPTH, D_MODEL)),
        'norm_mlp': gain(ks[3], (DEPTH, D_MODEL)),
        'norm_ple': gain(ks[4], (DEPTH, D_MODEL)),
        'norm_final': gain(ks[5], (D_MODEL,)),
        'mlstm_w_in': nrm(ks[6], (nA, D_MODEL, MLSTM_IN_WIDTH), D_MODEL),
        'mlstm_b_gates': gate_base + 0.1 * jax.random.normal(ks[7], (nA, 4, MLSTM_HEADS), jnp.float32),
        'mlstm_head_norm': gain(ks[8], (nA, V_WIDTH)),
        'mlstm_w_out': nrm(ks[9], (nA, V_WIDTH, D_MODEL), V_WIDTH),
        'pool_w_in': nrm(ks[10], (nB, D_MODEL, D_MODEL), D_MODEL),
        'pool_w_grp': nrm(ks[11], (nB, POOL_GROUPS, POOL_GROUP_WIDTH, POOL_GROUP_WIDTH), POOL_GROUP_WIDTH),
        'pool_scale': gain(ks[12], (nB, D_MODEL)),
        'pool_w_out': nrm(ks[13], (nB, D_MODEL, D_MODEL), D_MODEL),
        'mlp_w1': nrm(ks[14], (DEPTH, D_MODEL, D_FF), D_MODEL),
        'mlp_w2': nrm(ks[15], (DEPTH, D_FF, D_MODEL), D_FF),
        'ple_w': nrm(ks[16], (DEPTH, PLE_DIM, D_MODEL), PLE_DIM),
        'ple_gate_w': nrm(ks[17], (DEPTH, D_MODEL, D_MODEL), D_MODEL),
        'ple_gate_b': 0.02 * jax.random.normal(ks[18], (DEPTH, D_MODEL), jnp.float32),
    }


def reference(x, p, norm_mix, norm_mlp, norm_ple, norm_final,
              mlstm_w_in, mlstm_b_gates, mlstm_head_norm, mlstm_w_out,
              pool_w_in, pool_w_grp, pool_scale, pool_w_out,
              mlp_w1, mlp_w2, ple_w, ple_gate_w, ple_gate_b):
    h = x
    for i in range(DEPTH):
        xn = rmsnorm(h, norm_mix[i])
        j = i // N_MIXERS
        if i % N_MIXERS == 0:
            mix = mlstm_mixer(xn, mlstm_w_in[j], mlstm_b_gates[j], mlstm_head_norm[j], mlstm_w_out[j])
        else:
            mix = pool_mixer(xn, pool_w_in[j], pool_w_grp[j], pool_scale[j], pool_w_out[j])
        h = h + mix
        h = h + sqrelu_mlp(rmsnorm(h, norm_mlp[i]), mlp_w1[i], mlp_w2[i])
        gate = jax.nn.sigmoid((rmsnorm(h, norm_ple[i]) @ ple_gate_w[i] + ple_gate_b[i]).astype(jnp.float32))
        h = h + (gate * (p[i] @ ple_w[i]).astype(jnp.float32)).astype(h.dtype)
    return rmsnorm(h, norm_final)
```

```python
import functools

import jax
import jax.numpy as jnp
from jax import lax
from jax.experimental import pallas as pl
from jax.experimental.pallas import tpu as pltpu

EPS = 1e-6
HEADS = 4
DK = 128
DV = 256
CHUNK = 128
QK_WIDTH = HEADS * DK
V_WIDTH = HEADS * DV
N_GATES = 4 * HEADS
POOL_WINDOWS = (2, 4, 8, 16)
POOL_HALO = 8
N_PREP = 8

LANES = 128
SUBLANES = 8
V7X_SCOPED_VMEM_BYTES = 60000 * 1024

BF = jnp.bfloat16
F32 = jnp.float32


def _dot(a, b):
    return jnp.dot(a, b, preferred_element_type=F32)


def _rmsnorm(x, gain):
    ms = jnp.mean(x * x, axis=-1, keepdims=True)
    return x * lax.rsqrt(ms + EPS) * gain


def _resident(shape):
    nd = len(shape)
    return pl.BlockSpec(shape, lambda *_: (0,) * nd, pipeline_mode=pl.Buffered(1))


def _rows(tm, width):
    return pl.BlockSpec((tm, width), lambda i: (i, 0))


def _params(sem, vmem=None):
    return pltpu.CompilerParams(dimension_semantics=sem, vmem_limit_bytes=vmem)


def _inproj_body(x_ref, gain_ref, w_ref, wg_ref, q_ref, k_ref, v_ref, so_ref, g_ref):
    xn = _rmsnorm(x_ref[...], gain_ref[...]).astype(BF)
    q_ref[...] = _dot(xn, w_ref[:, 0:QK_WIDTH]).astype(BF)
    k_ref[...] = (_dot(xn, w_ref[:, QK_WIDTH:2 * QK_WIDTH]) * (DK ** -0.5)).astype(BF)
    v_ref[...] = _dot(xn, w_ref[:, 2 * QK_WIDTH:2 * QK_WIDTH + V_WIDTH]).astype(BF)
    o = _dot(xn, w_ref[:, 2 * QK_WIDTH + V_WIDTH:2 * QK_WIDTH + 2 * V_WIDTH])
    so_ref[...] = jax.nn.sigmoid(o).astype(BF)
    g_ref[...] = _dot(xn, wg_ref[...])[:, 0:N_GATES]


def _inproj(x2, gain, w_main, w_gate, tm):
    m, d = x2.shape
    return pl.pallas_call(
        _inproj_body,
        grid=(m // tm,),
        in_specs=[_rows(tm, d), _resident(gain.shape), _resident(w_main.shape), _resident(w_gate.shape)],
        out_specs=[_rows(tm, QK_WIDTH), _rows(tm, QK_WIDTH), _rows(tm, V_WIDTH), _rows(tm, V_WIDTH),
                   _rows(tm, N_GATES)],
        out_shape=[jax.ShapeDtypeStruct((m, QK_WIDTH), BF), jax.ShapeDtypeStruct((m, QK_WIDTH), BF),
                   jax.ShapeDtypeStruct((m, V_WIDTH), BF), jax.ShapeDtypeStruct((m, V_WIDTH), BF),
                   jax.ShapeDtypeStruct((m, N_GATES), F32)],
        compiler_params=_params(("parallel",), V7X_SCOPED_VMEM_BYTES),
        name="mlstm_inproj",
    )(x2, gain, w_main, w_gate)


def _log_sigmoid(x):
    return jnp.minimum(x, 0.0) - jnp.log1p(jnp.exp(-jnp.abs(x)))


def _split3(x):
    hi = x.astype(BF)
    r1 = x - hi.astype(F32)
    mid = r1.astype(BF)
    lo = (r1 - mid.astype(F32)).astype(BF)
    return hi, mid, lo


def _prep_body(gt_ref, bias_ref, out_ref, mprev_ref):
    nc = gt_ref.shape[1]
    L = CHUNK
    g = gt_ref[0] + bias_ref[...][None]
    i_pre = g[:, 0:8, :].reshape(nc * 8, L)
    logf = _log_sigmoid(g[:, 8:16, :]).reshape(nc * 8, L)

    row = lax.broadcasted_iota(jnp.int32, (nc * 8, L), 0)
    lane = lax.broadcasted_iota(jnp.int32, (nc * 8, L), 1)
    is_fwd = (row % 8) < HEADS

    s_idx = lax.broadcasted_iota(jnp.int32, (L, L), 0)
    t_idx = lax.broadcasted_iota(jnp.int32, (L, L), 1)
    tri_pre = (s_idx <= t_idx).astype(BF)
    tri_suf = (s_idx >= t_idx).astype(BF)
    pre = jnp.zeros((nc * 8, L), F32)
    suf = jnp.zeros((nc * 8, L), F32)
    for piece in _split3(logf):
        pre = pre + _dot(piece, tri_pre)
        suf = suf + _dot(piece, tri_suf)
    b = jnp.where(is_fwd, pre, suf)
    a = i_pre - b

    pm = a
    sm = a
    k = 1
    while k < L:
        pm = jnp.where(lane >= k, jnp.maximum(pm, pltpu.roll(pm, k, 1)), pm)
        sm = jnp.where(lane < L - k, jnp.maximum(sm, pltpu.roll(sm, L - k, 1)), sm)
        k *= 2
    cm = jnp.where(is_fwd, pm, sm)

    amax = jnp.broadcast_to(jnp.max(a, axis=1, keepdims=True), (nc * 8, L)).reshape(nc, 8, L)
    b_last = jnp.broadcast_to(jnp.sum(logf, axis=1, keepdims=True), (nc * 8, L)).reshape(nc, 8, L)

    row8 = lax.broadcasted_iota(jnp.int32, (8, L), 0) < HEADS
    m = jnp.zeros((8, L), F32)
    for c in range(nc):
        cb = nc - 1 - c
        mprev_ref[c, 0:HEADS, :] = m[0:HEADS]
        mprev_ref[cb, HEADS:8, :] = m[HEADS:8]
        am = jnp.where(row8, amax[c], amax[cb])
        bl = jnp.where(row8, b_last[c], b_last[cb])
        m = bl + jnp.maximum(m, am)

    mprev = mprev_ref[...]
    a3 = a.reshape(nc, 8, L)
    b3 = b.reshape(nc, 8, L)
    cm3 = cm.reshape(nc, 8, L)
    sigma = jnp.maximum(mprev, amax)
    mm = jnp.maximum(mprev, cm3)
    out_ref[0, 0] = a3
    out_ref[0, 1] = mprev
    out_ref[0, 2] = mm
    out_ref[0, 3] = jnp.exp(-(b3 + mm))
    out_ref[0, 4] = jnp.exp(a3 - sigma)
    out_ref[0, 5] = jnp.exp(mprev - sigma)
    out_ref[0, 6] = jnp.zeros((nc, 8, L), F32)
    out_ref[0, 7] = jnp.zeros((nc, 8, L), F32)


def _prep(gt, bias_rows):
    bsz, nc, _, L = gt.shape
    return pl.pallas_call(
        _prep_body,
        grid=(bsz,),
        in_specs=[pl.BlockSpec((1, nc, N_GATES, L), lambda b: (b, 0, 0, 0)), _resident(bias_rows.shape)],
        out_specs=pl.BlockSpec((1, N_PREP, nc, 8, L), lambda b: (b, 0, 0, 0, 0)),
        out_shape=jax.ShapeDtypeStruct((bsz, N_PREP, nc, 8, L), F32),
        scratch_shapes=[pltpu.VMEM((nc, 8, L), F32)],
        compiler_params=_params(("parallel",)),
        name="mlstm_gate_prep",
    )(gt, bias_rows)


def _scan_body(q_ref, k_ref, v_ref, rows_ref, out_ref, acc_ref, st_ref):
    L = CHUNK
    nc = rows_ref.shape[3]
    half = nc // 2
    st_ref[...] = jnp.zeros(st_ref.shape, F32)

    t_idx = lax.broadcasted_iota(jnp.int32, (L, 2 * L), 0)
    s_idx = lax.broadcasted_iota(jnp.int32, (L, 2 * L), 1)
    masks = ((s_idx <= t_idx) | (s_idx >= L), (s_idx >= t_idx))
    ones_blk = jnp.ones((L, L), BF)

    def chunk(d, c):
        r0 = pl.multiple_of(c * L, L)
        qc = q_ref[0, pl.ds(r0, L), :]
        kc = k_ref[0, pl.ds(r0, L), :]
        vc = v_ref[0, pl.ds(r0, L), :]
        tile = rows_ref[0, d, 0, c]
        cols = tile.T
        rowvec = jnp.concatenate([tile[0:1], tile[1:2]], axis=1)
        mm, eneg, w = cols[:, 2:3], cols[:, 3:4], cols[:, 4:5]
        decay = tile[5:6]

        s = lax.dot_general(qc, kc, (((1,), (1,)), ((), ())), preferred_element_type=F32)
        dmat = jnp.where(masks[d], jnp.exp(rowvec - mm), 0.0)
        lhs = (jnp.concatenate([s, qc.astype(F32)], axis=1) * dmat).astype(BF)
        st = st_ref[d]
        rhs = jnp.concatenate([jnp.concatenate([vc, ones_blk], axis=1), st.astype(BF)], axis=0)
        main = _dot(lhs, rhs)
        r = 1.0 / jnp.maximum(jnp.abs(main[:, DV:]), eneg)
        h = main[:, 0:DV] * jnp.concatenate([r, r], axis=1)

        wb = jnp.broadcast_to(w, (L, L))
        wv = jnp.concatenate([vc.astype(F32) * jnp.concatenate([wb, wb], axis=1), wb], axis=1).astype(BF)
        upd = lax.dot_general(kc, wv, (((0,), (0,)), ((), ())), preferred_element_type=F32)
        st_ref[d] = jnp.concatenate([decay, decay, decay], axis=1) * st + upd
        return h

    def first(c, carry):
        cb = nc - 1 - c
        acc_ref[pl.ds(pl.multiple_of(c * L, L), L), :] = chunk(0, c)
        acc_ref[pl.ds(pl.multiple_of(cb * L, L), L), :] = chunk(1, cb)
        return carry

    def second(c, carry):
        cb = nc - 1 - c
        rf = pl.ds(pl.multiple_of(c * L, L), L)
        rb = pl.ds(pl.multiple_of(cb * L, L), L)
        out_ref[0, rf, :] = (acc_ref[rf, :] + chunk(0, c)).astype(BF)
        out_ref[0, rb, :] = (acc_ref[rb, :] + chunk(1, cb)).astype(BF)
        return carry

    lax.fori_loop(0, half, first, 0)
    lax.fori_loop(half, nc, second, 0)


def _scan(q, k, v, rows):
    bsz, s, _ = q.shape
    nc = s // CHUNK
    assert nc % 2 == 0
    return pl.pallas_call(
        _scan_body,
        grid=(bsz, HEADS),
        in_specs=[pl.BlockSpec((1, s, DK), lambda b, h: (b, 0, h)),
                  pl.BlockSpec((1, s, DK), lambda b, h: (b, 0, h)),
                  pl.BlockSpec((1, s, DV), lambda b, h: (b, 0, h)),
                  pl.BlockSpec((1, 2, 1, nc, 8, CHUNK), lambda b, h: (b, 0, h, 0, 0, 0))],
        out_specs=pl.BlockSpec((1, s, DV), lambda b, h: (b, 0, h)),
        out_shape=jax.ShapeDtypeStruct((bsz, s, V_WIDTH), BF),
        scratch_shapes=[pltpu.VMEM((s, DV), F32), pltpu.VMEM((2, DK, DV + CHUNK), F32)],
        compiler_params=_params(("parallel", "parallel"), V7X_SCOPED_VMEM_BYTES),
        name="mlstm_scan",
    )(q, k, v, rows)


def _mlp(xn, w1_ref, w2_ref, ff_chunk):
    d_ff = w1_ref.shape[1]
    acc = None
    for c0 in range(0, d_ff, ff_chunk):
        a = _dot(xn, w1_ref[:, c0:c0 + ff_chunk])
        a = jnp.square(jnp.maximum(a, 0.0)).astype(BF)
        part = _dot(a, w2_ref[c0:c0 + ff_chunk, :])
        acc = part if acc is None else acc + part
    return acc


def _mlp_and_embed(h, p_ref, nmlp_ref, w1_ref, w2_ref, nple_ref, gw_ref, gb_ref, plew_ref, ff_chunk):
    h = h + _mlp(_rmsnorm(h, nmlp_ref[...]).astype(BF), w1_ref, w2_ref, ff_chunk)
    gate = jax.nn.sigmoid(_dot(_rmsnorm(h, nple_ref[...]).astype(BF), gw_ref[...]) + gb_ref[...])
    return h + gate * _dot(p_ref[...].astype(BF), plew_ref[...])


def _tail0_body(hs_ref, so_ref, x_ref, p_ref, hnorm_ref, wout_ref, nmlp_ref, w1_ref, w2_ref, nple_ref,
                gw_ref, gb_ref, plew_ref, nmix_ref, pin_ref, h_ref, u_ref, *, ff_chunk):
    hs = hs_ref[...].astype(F32)
    parts = []
    for hd in range(HEADS):
        blk = hs[:, hd * DV:(hd + 1) * DV]
        ms = jnp.mean(blk * blk, axis=-1, keepdims=True)
        parts.append(blk * lax.rsqrt(ms + EPS))
    hn = jnp.concatenate(parts, axis=1) * hnorm_ref[...] * so_ref[...].astype(F32)
    h = x_ref[...] + _dot(hn.astype(BF), wout_ref[...])
    h = _mlp_and_embed(h, p_ref, nmlp_ref, w1_ref, w2_ref, nple_ref, gw_ref, gb_ref, plew_ref, ff_chunk)
    h_ref[...] = h
    u_ref[...] = _dot(_rmsnorm(h, nmix_ref[...]).astype(BF), pin_ref[...])


def _tail0(hs, so, x2, p2, weights, tm, ff_chunk):
    m, d = x2.shape
    return pl.pallas_call(
        functools.partial(_tail0_body, ff_chunk=ff_chunk),
        grid=(m // tm,),
        in_specs=[_rows(tm, d), _rows(tm, d), _rows(tm, d), _rows(tm, p2.shape[1])]
                 + [_resident(w.shape) for w in weights],
        out_specs=[_rows(tm, d), _rows(tm, d)],
        out_shape=[jax.ShapeDtypeStruct((m, d), F32), jax.ShapeDtypeStruct((m, d), F32)],
        compiler_params=_params(("parallel",), V7X_SCOPED_VMEM_BYTES),
        name="layer0_tail",
    )(hs, so, x2, p2, *weights)


def _layer1_body(u_ref, uprev_ref, unext_ref, h_ref, p_ref, wgrp_ref, scale_ref, pout_ref, nmlp_ref, w1_ref,
                 w2_ref, nple_ref, gw_ref, gb_ref, plew_ref, nfin_ref, out_ref, ext_ref, *, seq, ff_chunk):
    tm, d = u_ref.shape
    gw = d // len(POOL_WINDOWS)
    t0 = (pl.program_id(0) * tm) % seq
    ext_ref[0:POOL_HALO, :] = jnp.where(t0 > 0, uprev_ref[...], 0.0)
    ext_ref[POOL_HALO:POOL_HALO + tm, :] = u_ref[...]
    ext_ref[POOL_HALO + tm:, :] = jnp.where(t0 + tm < seq, unext_ref[...], 0.0)

    t = t0 + lax.broadcasted_iota(jnp.int32, (tm, gw), 0)
    mixed = []
    for gi, win in enumerate(POOL_WINDOWS):
        cols = slice(gi * gw, (gi + 1) * gw)
        lo_off = win // 2
        total = None
        for j in range(-lo_off, win - lo_off):
            sl = ext_ref[pl.ds(POOL_HALO + j, tm), cols]
            total = sl if total is None else total + sl
        cnt = jnp.minimum(t + (win - lo_off), seq) - jnp.maximum(t - lo_off, 0)
        y = total / cnt.astype(F32) - u_ref[:, cols]
        mixed.append(_dot(y.astype(BF), wgrp_ref[gi]))
    y = jnp.concatenate(mixed, axis=1) * scale_ref[...]
    h = h_ref[...] + _dot(y.astype(BF), pout_ref[...])
    h = _mlp_and_embed(h, p_ref, nmlp_ref, w1_ref, w2_ref, nple_ref, gw_ref, gb_ref, plew_ref, ff_chunk)
    out_ref[...] = _rmsnorm(h, nfin_ref[...])


def _layer1(u, h, p2, weights, tm, seq, ff_chunk):
    m, d = u.shape
    halo_blocks_per_tile = tm // POOL_HALO
    last_halo_block = m // POOL_HALO - 1
    return pl.pallas_call(
        functools.partial(_layer1_body, seq=seq, ff_chunk=ff_chunk),
        grid=(m // tm,),
        in_specs=[_rows(tm, d),
                  pl.BlockSpec((POOL_HALO, d), lambda i: (jnp.maximum(i * halo_blocks_per_tile - 1, 0), 0)),
                  pl.BlockSpec((POOL_HALO, d),
                               lambda i: (jnp.minimum((i + 1) * halo_blocks_per_tile, last_halo_block), 0)),
                  _rows(tm, d), _rows(tm, p2.shape[1])]
                 + [_resident(w.shape) for w in weights],
        out_specs=_rows(tm, d),
        out_shape=jax.ShapeDtypeStruct((m, d), F32),
        scratch_shapes=[pltpu.VMEM((tm + 2 * POOL_HALO, d), F32)],
        compiler_params=_params(("parallel",), V7X_SCOPED_VMEM_BYTES),
        name="layer1_pool_mlp",
    )(u, u, u, h, p2, *weights)


def kernel(x, p, norm_mix, norm_mlp, norm_ple, norm_final, mlstm_w_in, mlstm_b_gates, mlstm_head_norm,
           mlstm_w_out, pool_w_in, pool_w_grp, pool_scale, pool_w_out, mlp_w1, mlp_w2, ple_w, ple_gate_w,
           ple_gate_b):
    bsz, seq, d = x.shape
    m = bsz * seq
    nc = seq // CHUNK
    tm_proj, tm_fused, ff_chunk = 512, 256, 1024
    assert seq % CHUNK == 0 and seq % tm_fused == 0 and m % tm_proj == 0
    row = lambda v: v.reshape(1, -1)
    bf = lambda w: w.astype(BF)

    x2 = x.reshape(m, d)
    w_in = mlstm_w_in[0]
    n_main = 2 * QK_WIDTH + 2 * V_WIDTH
    gate_perm = jnp.array([0, 1, 2, 3, 8, 9, 10, 11, 4, 5, 6, 7, 12, 13, 14, 15], jnp.int32)
    w_gate = jnp.pad(w_in[:, n_main:][:, gate_perm], ((0, 0), (0, LANES - N_GATES)))
    bias_rows = jnp.broadcast_to(mlstm_b_gates[0].reshape(N_GATES)[gate_perm][:, None], (N_GATES, CHUNK))

    q, k, v, so, g = _inproj(x2, row(norm_mix[0]), bf(w_in[:, :n_main]), bf(w_gate), tm_proj)

    gt = g.reshape(bsz, nc, CHUNK, N_GATES).transpose(0, 1, 3, 2)
    prep = _prep(gt, bias_rows)
    rows = prep.transpose(0, 3, 2, 1, 4).reshape(bsz, 2, HEADS, nc, N_PREP, CHUNK)

    hs = _scan(q.reshape(bsz, seq, QK_WIDTH), k.reshape(bsz, seq, QK_WIDTH), v.reshape(bsz, seq, V_WIDTH), rows)

    tail_weights = [row(mlstm_head_norm[0]), bf(mlstm_w_out[0]), row(norm_mlp[0]), bf(mlp_w1[0]),
                    bf(mlp_w2[0]), row(norm_ple[0]), bf(ple_gate_w[0]), row(ple_gate_b[0]), bf(ple_w[0]),
                    row(norm_mix[1]), bf(pool_w_in[0])]
    h, u = _tail0(hs.reshape(m, V_WIDTH), so, x2, p[0].reshape(m, -1), tail_weights, tm_fused, ff_chunk)

    l1_weights = [bf(pool_w_grp[0]), row(pool_scale[0]), bf(pool_w_out[0]), row(norm_mlp[1]), bf(mlp_w1[1]),
                  bf(mlp_w2[1]), row(norm_ple[1]), bf(ple_gate_w[1]), row(ple_gate_b[1]), bf(ple_w[1]),
                  row(norm_final)]
    out = _layer1(u, h, p[1].reshape(m, -1), l1_weights, tm_fused, seq, ff_chunk)
    return out.reshape(bsz, seq, d)
```

```python
import functools

import jax
import jax.numpy as jnp
from jax import lax
from jax.experimental import pallas as pl
from jax.experimental.pallas import tpu as pltpu

EPS = 1e-6
HEADS = 4
DK = 128
DV = 256
CHUNK = 128
QK_WIDTH = HEADS * DK
V_WIDTH = HEADS * DV
N_GATES = 4 * HEADS
POOL_WINDOWS = (2, 4, 8, 16)
POOL_HALO = 8
N_PREP = 8

LANES = 128
SUBLANES = 8
V7X_SCOPED_VMEM_BYTES = 60000 * 1024

LOG2E = 1.4426950408889634

BF = jnp.bfloat16
F32 = jnp.float32


def _dot(a, b):
    return jnp.dot(a, b, preferred_element_type=F32)


def _rmsnorm(x, gain):
    ms = jnp.mean(x * x, axis=-1, keepdims=True)
    return x * lax.rsqrt(ms + EPS) * gain


def _resident(shape):
    nd = len(shape)
    return pl.BlockSpec(shape, lambda *_: (0,) * nd, pipeline_mode=pl.Buffered(1))


def _rows(tm, width):
    return pl.BlockSpec((tm, width), lambda i: (i, 0))


def _params(sem, vmem=None):
    return pltpu.CompilerParams(dimension_semantics=sem, vmem_limit_bytes=vmem)


def _inproj_body(x_ref, gain_ref, wq_ref, wkt_ref, wvo_ref, wg_ref, q_ref, kt_ref, v_ref, so_ref, g_ref):
    xn = _rmsnorm(x_ref[...], gain_ref[...]).astype(BF)
    q_ref[...] = _dot(xn, wq_ref[...]).astype(BF)
    kt = lax.dot_general(wkt_ref[...], xn, (((1,), (1,)), ((), ())), preferred_element_type=F32)
    kt_ref[0] = (kt * (DK ** -0.5)).astype(BF)
    v_ref[...] = _dot(xn, wvo_ref[:, 0:V_WIDTH]).astype(BF)
    so_ref[...] = jax.nn.sigmoid(_dot(xn, wvo_ref[:, V_WIDTH:2 * V_WIDTH])).astype(BF)
    g_ref[...] = _dot(xn, wg_ref[...])[:, 0:N_GATES]


def _inproj(x2, gain, w_q, w_kt, w_vo, w_gate, tm, seq):
    m, d = x2.shape
    tiles_per_seq = seq // tm
    weights = [gain, w_q, w_kt, w_vo, w_gate]
    return pl.pallas_call(
        _inproj_body,
        grid=(m // tm,),
        in_specs=[_rows(tm, d)] + [_resident(w.shape) for w in weights],
        out_specs=[_rows(tm, QK_WIDTH),
                   pl.BlockSpec((1, QK_WIDTH, tm), lambda i: (i // tiles_per_seq, 0, i % tiles_per_seq)),
                   _rows(tm, V_WIDTH), _rows(tm, V_WIDTH), _rows(tm, N_GATES)],
        out_shape=[jax.ShapeDtypeStruct((m, QK_WIDTH), BF), jax.ShapeDtypeStruct((m // seq, QK_WIDTH, seq), BF),
                   jax.ShapeDtypeStruct((m, V_WIDTH), BF), jax.ShapeDtypeStruct((m, V_WIDTH), BF),
                   jax.ShapeDtypeStruct((m, N_GATES), F32)],
        compiler_params=_params(("parallel",), V7X_SCOPED_VMEM_BYTES),
        name="mlstm_inproj",
    )(x2, *weights)


def _log_sigmoid(x):
    return jnp.minimum(x, 0.0) - jnp.log1p(jnp.exp(-jnp.abs(x)))


def _split3(x):
    hi = x.astype(BF)
    r1 = x - hi.astype(F32)
    mid = r1.astype(BF)
    lo = (r1 - mid.astype(F32)).astype(BF)
    return hi, mid, lo


def _prep_body(gt_ref, bias_ref, out_ref, mprev_ref):
    nc = gt_ref.shape[1]
    L = CHUNK
    g = gt_ref[0] + bias_ref[...][None]
    i_pre = g[:, 0:8, :].reshape(nc * 8, L)
    logf = _log_sigmoid(g[:, 8:16, :]).reshape(nc * 8, L)

    row = lax.broadcasted_iota(jnp.int32, (nc * 8, L), 0)
    lane = lax.broadcasted_iota(jnp.int32, (nc * 8, L), 1)
    is_fwd = (row % 8) < HEADS

    s_idx = lax.broadcasted_iota(jnp.int32, (L, L), 0)
    t_idx = lax.broadcasted_iota(jnp.int32, (L, L), 1)
    tri_pre = (s_idx <= t_idx).astype(BF)
    tri_suf = (s_idx >= t_idx).astype(BF)
    pre = jnp.zeros((nc * 8, L), F32)
    suf = jnp.zeros((nc * 8, L), F32)
    for piece in _split3(logf):
        pre = pre + _dot(piece, tri_pre)
        suf = suf + _dot(piece, tri_suf)
    b = jnp.where(is_fwd, pre, suf)
    a = i_pre - b

    pm = a
    sm = a
    k = 1
    while k < L:
        pm = jnp.where(lane >= k, jnp.maximum(pm, pltpu.roll(pm, k, 1)), pm)
        sm = jnp.where(lane < L - k, jnp.maximum(sm, pltpu.roll(sm, L - k, 1)), sm)
        k *= 2
    cm = jnp.where(is_fwd, pm, sm)

    amax = jnp.broadcast_to(jnp.max(a, axis=1, keepdims=True), (nc * 8, L)).reshape(nc, 8, L)
    b_last = jnp.broadcast_to(jnp.sum(logf, axis=1, keepdims=True), (nc * 8, L)).reshape(nc, 8, L)

    row8 = lax.broadcasted_iota(jnp.int32, (8, L), 0) < HEADS
    m = jnp.zeros((8, L), F32)
    for c in range(nc):
        cb = nc - 1 - c
        mprev_ref[c, 0:HEADS, :] = m[0:HEADS]
        mprev_ref[cb, HEADS:8, :] = m[HEADS:8]
        am = jnp.where(row8, amax[c], amax[cb])
        bl = jnp.where(row8, b_last[c], b_last[cb])
        m = bl + jnp.maximum(m, am)

    mprev = mprev_ref[...]
    a3 = a.reshape(nc, 8, L)
    b3 = b.reshape(nc, 8, L)
    cm3 = cm.reshape(nc, 8, L)
    sigma = jnp.maximum(mprev, amax)
    mm = jnp.maximum(mprev, cm3)
    out_ref[0, 0] = a3 * LOG2E
    out_ref[0, 1] = mprev * LOG2E
    out_ref[0, 2] = mm * LOG2E
    out_ref[0, 3] = jnp.exp(-(b3 + mm))
    out_ref[0, 4] = jnp.exp(a3 - sigma)
    out_ref[0, 5] = jnp.exp(mprev - sigma)
    out_ref[0, 6] = jnp.zeros((nc, 8, L), F32)
    out_ref[0, 7] = jnp.zeros((nc, 8, L), F32)


def _prep(gt, bias_rows):
    bsz, nc, _, L = gt.shape
    return pl.pallas_call(
        _prep_body,
        grid=(bsz,),
        in_specs=[pl.BlockSpec((1, nc, N_GATES, L), lambda b: (b, 0, 0, 0)), _resident(bias_rows.shape)],
        out_specs=pl.BlockSpec((1, N_PREP, nc, 8, L), lambda b: (b, 0, 0, 0, 0)),
        out_shape=jax.ShapeDtypeStruct((bsz, N_PREP, nc, 8, L), F32),
        scratch_shapes=[pltpu.VMEM((nc, 8, L), F32)],
        compiler_params=_params(("parallel",)),
        name="mlstm_gate_prep",
    )(gt, bias_rows)


def _scan_body(qf_ref, ktf_ref, vf_ref, pf_ref, qb_ref, ktb_ref, vb_ref, pb_ref, hf_ref, hb_ref, st_ref):
    L = CHUNK
    cb = pf_ref.shape[2]

    @pl.when(pl.program_id(1) == 0)
    def _():
        st_ref[...] = jnp.zeros(st_ref.shape, F32)

    t_idx = lax.broadcasted_iota(jnp.int32, (L, 2 * L), 0)
    s_idx = lax.broadcasted_iota(jnp.int32, (L, 2 * L), 1)
    masks = ((s_idx <= t_idx) | (s_idx >= L), (s_idx >= t_idx))
    ones_blk = jnp.ones((L, L), BF)

    dirs = ((qf_ref, ktf_ref, vf_ref, pf_ref, hf_ref), (qb_ref, ktb_ref, vb_ref, pb_ref, hb_ref))

    def body(ci, carry):
        probs = []
        for d, (q_ref, kt_ref, v_ref, p_ref, h_ref) in enumerate(dirs):
            c = ci if d == 0 else cb - 1 - ci
            rows = pl.ds(pl.multiple_of(c * L, L), L)
            a2, mprev2, mm2, eneg, w, decay = [p_ref[0, qi, c] for qi in range(6)]
            mm2_cols = mm2.T
            eneg_cols = eneg.T
            for hd in range(HEADS):
                rix = d * HEADS + hd
                qc = q_ref[0, rows, hd * DK:(hd + 1) * DK]
                ktc = kt_ref[0, hd * DK:(hd + 1) * DK, rows]
                vc = v_ref[0, rows, hd * DV:(hd + 1) * DV]
                probs.append(dict(
                    d=d, hd=hd, rows=rows, h_ref=h_ref, qc=qc, ktc=ktc, s=_dot(qc, ktc),
                    top=jnp.concatenate([vc, ones_blk], axis=1),
                    rowvec=jnp.concatenate([a2[rix:rix + 1], mprev2[rix:rix + 1]], axis=1),
                    mm2=mm2_cols[:, rix:rix + 1], eneg=eneg_cols[:, rix:rix + 1],
                    w=w[rix:rix + 1], dec=decay[rix:rix + 1]))
        for pr in probs:
            dmat = jnp.where(masks[pr["d"]], jnp.exp2(pr["rowvec"] - pr["mm2"]), 0.0)
            lhs = (jnp.concatenate([pr["s"], pr["qc"].astype(F32)], axis=1) * dmat).astype(BF)
            st = st_ref[pr["d"], pr["hd"]]
            pr["main"] = _dot(lhs, jnp.concatenate([pr["top"], st.astype(BF)], axis=0))
            ktw = (pr["ktc"].astype(F32) * pr["w"]).astype(BF)
            dec = pr["dec"]
            pr["st_new"] = jnp.concatenate([dec, dec, dec], axis=1) * st + _dot(ktw, pr["top"])
        for pr in probs:
            main = pr["main"]
            r = 1.0 / jnp.maximum(jnp.abs(main[:, DV:]), pr["eneg"])
            h = main[:, 0:DV] * jnp.concatenate([r, r], axis=1)
            pr["h_ref"][0, pr["rows"], pr["hd"] * DV:(pr["hd"] + 1) * DV] = h.astype(BF)
            st_ref[pr["d"], pr["hd"]] = pr["st_new"]
        return carry

    lax.fori_loop(0, cb, body, 0)


def _scan(q, kt, v, prep, sb):
    bsz, s, _ = q.shape
    nb = s // sb
    cb = sb // CHUNK
    fwd3 = lambda b, j: (b, j, 0)
    bwd3 = lambda b, j: (b, nb - 1 - j, 0)
    in_specs = []
    for blk3, blkt, blkp in ((fwd3, lambda b, j: (b, 0, j), lambda b, j: (b, 0, j, 0, 0)),
                             (bwd3, lambda b, j: (b, 0, nb - 1 - j), lambda b, j: (b, 0, nb - 1 - j, 0, 0))):
        in_specs += [pl.BlockSpec((1, sb, QK_WIDTH), blk3), pl.BlockSpec((1, QK_WIDTH, sb), blkt),
                     pl.BlockSpec((1, sb, V_WIDTH), blk3), pl.BlockSpec((1, N_PREP, cb, 8, CHUNK), blkp)]
    return pl.pallas_call(
        _scan_body,
        grid=(bsz, nb),
        in_specs=in_specs,
        out_specs=[pl.BlockSpec((1, sb, V_WIDTH), fwd3), pl.BlockSpec((1, sb, V_WIDTH), bwd3)],
        out_shape=[jax.ShapeDtypeStruct((bsz, s, V_WIDTH), BF)] * 2,
        scratch_shapes=[pltpu.VMEM((2, HEADS, DK, DV + CHUNK), F32)],
        compiler_params=_params(("parallel", "arbitrary"), V7X_SCOPED_VMEM_BYTES),
        name="mlstm_scan",
    )(q, kt, v, prep, q, kt, v, prep)


def _mlp(xn, w1_ref, w2_ref, ff_chunk):
    d_ff = w1_ref.shape[1]
    acc = None
    for c0 in range(0, d_ff, ff_chunk):
        a = _dot(xn, w1_ref[:, c0:c0 + ff_chunk])
        a = jnp.square(jnp.maximum(a, 0.0)).astype(BF)
        part = _dot(a, w2_ref[c0:c0 + ff_chunk, :])
        acc = part if acc is None else acc + part
    return acc


def _mlp_and_embed(h, p_ref, nmlp_ref, w1_ref, w2_ref, nple_ref, gw_ref, gb_ref, plew_ref, ff_chunk):
    h = h + _mlp(_rmsnorm(h, nmlp_ref[...]).astype(BF), w1_ref, w2_ref, ff_chunk)
    gate = jax.nn.sigmoid(_dot(_rmsnorm(h, nple_ref[...]).astype(BF), gw_ref[...]) + gb_ref[...])
    return h + gate * _dot(p_ref[...].astype(BF), plew_ref[...])


def _tail0_body(hf_ref, hb_ref, so_ref, x_ref, p_ref, hnorm_ref, wout_ref, nmlp_ref, w1_ref, w2_ref, nple_ref,
                gw_ref, gb_ref, plew_ref, nmix_ref, pin_ref, h_ref, u_ref, *, ff_chunk):
    hs = hf_ref[...].astype(F32) + hb_ref[...].astype(F32)
    parts = []
    for hd in range(HEADS):
        blk = hs[:, hd * DV:(hd + 1) * DV]
        ms = jnp.mean(blk * blk, axis=-1, keepdims=True)
        parts.append(blk * lax.rsqrt(ms + EPS))
    hn = jnp.concatenate(parts, axis=1) * hnorm_ref[...] * so_ref[...].astype(F32)
    h = x_ref[...] + _dot(hn.astype(BF), wout_ref[...])
    h = _mlp_and_embed(h, p_ref, nmlp_ref, w1_ref, w2_ref, nple_ref, gw_ref, gb_ref, plew_ref, ff_chunk)
    h_ref[...] = h
    u_ref[...] = _dot(_rmsnorm(h, nmix_ref[...]).astype(BF), pin_ref[...])


def _tail0(hf, hb, so, x2, p2, weights, tm, ff_chunk):
    m, d = x2.shape
    return pl.pallas_call(
        functools.partial(_tail0_body, ff_chunk=ff_chunk),
        grid=(m // tm,),
        in_specs=[_rows(tm, d), _rows(tm, d), _rows(tm, d), _rows(tm, d), _rows(tm, p2.shape[1])]
                 + [_resident(w.shape) for w in weights],
        out_specs=[_rows(tm, d), _rows(tm, d)],
        out_shape=[jax.ShapeDtypeStruct((m, d), F32), jax.ShapeDtypeStruct((m, d), F32)],
        compiler_params=_params(("parallel",), V7X_SCOPED_VMEM_BYTES),
        name="layer0_tail",
    )(hf, hb, so, x2, p2, *weights)


def _layer1_body(u_ref, uprev_ref, unext_ref, h_ref, p_ref, wgrp_ref, scale_ref, pout_ref, nmlp_ref, w1_ref,
                 w2_ref, nple_ref, gw_ref, gb_ref, plew_ref, nfin_ref, out_ref, ext_ref, *, seq, ff_chunk):
    tm, d = u_ref.shape
    gw = d // len(POOL_WINDOWS)
    t0 = (pl.program_id(0) * tm) % seq
    ext_ref[0:POOL_HALO, :] = jnp.where(t0 > 0, uprev_ref[...], 0.0)
    ext_ref[POOL_HALO:POOL_HALO + tm, :] = u_ref[...]
    ext_ref[POOL_HALO + tm:, :] = jnp.where(t0 + tm < seq, unext_ref[...], 0.0)

    t = t0 + lax.broadcasted_iota(jnp.int32, (tm, gw), 0)
    mixed = []
    for gi, win in enumerate(POOL_WINDOWS):
        cols = slice(gi * gw, (gi + 1) * gw)
        lo_off = win // 2
        total = None
        for j in range(-lo_off, win - lo_off):
            sl = ext_ref[pl.ds(POOL_HALO + j, tm), cols]
            total = sl if total is None else total + sl
        cnt = jnp.minimum(t + (win - lo_off), seq) - jnp.maximum(t - lo_off, 0)
        y = total / cnt.astype(F32) - u_ref[:, cols]
        mixed.append(_dot(y.astype(BF), wgrp_ref[gi]))
    y = jnp.concatenate(mixed, axis=1) * scale_ref[...]
    h = h_ref[...] + _dot(y.astype(BF), pout_ref[...])
    h = _mlp_and_embed(h, p_ref, nmlp_ref, w1_ref, w2_ref, nple_ref, gw_ref, gb_ref, plew_ref, ff_chunk)
    out_ref[...] = _rmsnorm(h, nfin_ref[...])


def _layer1(u, h, p2, weights, tm, seq, ff_chunk):
    m, d = u.shape
    halo_blocks_per_tile = tm // POOL_HALO
    last_halo_block = m // POOL_HALO - 1
    return pl.pallas_call(
        functools.partial(_layer1_body, seq=seq, ff_chunk=ff_chunk),
        grid=(m // tm,),
        in_specs=[_rows(tm, d),
                  pl.BlockSpec((POOL_HALO, d), lambda i: (jnp.maximum(i * halo_blocks_per_tile - 1, 0), 0)),
                  pl.BlockSpec((POOL_HALO, d),
                               lambda i: (jnp.minimum((i + 1) * halo_blocks_per_tile, last_halo_block), 0)),
                  _rows(tm, d), _rows(tm, p2.shape[1])]
                 + [_resident(w.shape) for w in weights],
        out_specs=_rows(tm, d),
        out_shape=jax.ShapeDtypeStruct((m, d), F32),
        scratch_shapes=[pltpu.VMEM((tm + 2 * POOL_HALO, d), F32)],
        compiler_params=_params(("parallel",), V7X_SCOPED_VMEM_BYTES),
        name="layer1_pool_mlp",
    )(u, u, u, h, p2, *weights)


def kernel(x, p, norm_mix, norm_mlp, norm_ple, norm_final, mlstm_w_in, mlstm_b_gates, mlstm_head_norm,
           mlstm_w_out, pool_w_in, pool_w_grp, pool_scale, pool_w_out, mlp_w1, mlp_w2, ple_w, ple_gate_w,
           ple_gate_b):
    bsz, seq, d = x.shape
    m = bsz * seq
    nc = seq // CHUNK
    tm_proj, tm_fused, ff_chunk, scan_block = min(512, seq), min(256, seq), 1024, min(1024, seq)
    assert seq % scan_block == 0 and seq % tm_fused == 0 and seq % tm_proj == 0 and scan_block % CHUNK == 0
    row = lambda v: v.reshape(1, -1)
    bf = lambda w: w.astype(BF)

    x2 = x.reshape(m, d)
    w_in = mlstm_w_in[0]
    n_main = 2 * QK_WIDTH + 2 * V_WIDTH
    gate_perm = jnp.array([0, 1, 2, 3, 8, 9, 10, 11, 4, 5, 6, 7, 12, 13, 14, 15], jnp.int32)
    w_gate = jnp.pad(w_in[:, n_main:][:, gate_perm], ((0, 0), (0, LANES - N_GATES)))
    bias_rows = jnp.broadcast_to(mlstm_b_gates[0].reshape(N_GATES)[gate_perm][:, None], (N_GATES, CHUNK))

    q, kt, v, so, g = _inproj(x2, row(norm_mix[0]), bf(w_in[:, 0:QK_WIDTH]), bf(w_in[:, QK_WIDTH:2 * QK_WIDTH].T),
                              bf(w_in[:, 2 * QK_WIDTH:n_main]), bf(w_gate), tm_proj, seq)

    gt = g.reshape(bsz, nc, CHUNK, N_GATES).transpose(0, 1, 3, 2)
    prep = _prep(gt, bias_rows)

    hf, hb = _scan(q.reshape(bsz, seq, QK_WIDTH), kt, v.reshape(bsz, seq, V_WIDTH), prep, scan_block)

    tail_weights = [row(mlstm_head_norm[0]), bf(mlstm_w_out[0]), row(norm_mlp[0]), bf(mlp_w1[0]),
                    bf(mlp_w2[0]), row(norm_ple[0]), bf(ple_gate_w[0]), row(ple_gate_b[0]), bf(ple_w[0]),
                    row(norm_mix[1]), bf(pool_w_in[0])]
    h, u = _tail0(hf.reshape(m, V_WIDTH), hb.reshape(m, V_WIDTH), so, x2, p[0].reshape(m, -1), tail_weights,
                  tm_fused, ff_chunk)

    l1_weights = [bf(pool_w_grp[0]), row(pool_scale[0]), bf(pool_w_out[0]), row(norm_mlp[1]), bf(mlp_w1[1]),
                  bf(mlp_w2[1]), row(norm_ple[1]), bf(ple_gate_w[1]), row(ple_gate_b[1]), bf(ple_w[1]),
                  row(norm_final)]
    out = _layer1(u, h, p[1].reshape(m, -1), l1_weights, tm_fused, seq, ff_chunk)
    return out.reshape(bsz, seq, d)
```

```python
import functools

import jax
import jax.numpy as jnp
from jax import lax
from jax.experimental import pallas as pl
from jax.experimental.pallas import tpu as pltpu

EPS = 1e-6
HEADS = 4
DK = 128
DV = 256
CHUNK = 128
QK_WIDTH = HEADS * DK
V_WIDTH = HEADS * DV
N_GATES = 4 * HEADS
POOL_WINDOWS = (2, 4, 8, 16)
POOL_HALO = 8
N_PREP = 8

LANES = 128
SUBLANES = 8
V7X_SCOPED_VMEM_BYTES = 60000 * 1024

LOG2E = 1.4426950408889634

BF = jnp.bfloat16
F32 = jnp.float32


def _dot(a, b):
    return jnp.dot(a, b, preferred_element_type=F32)


def _rmsnorm(x, gain):
    ms = jnp.mean(x * x, axis=-1, keepdims=True)
    return x * lax.rsqrt(ms + EPS) * gain


def _resident(shape):
    nd = len(shape)
    return pl.BlockSpec(shape, lambda *_: (0,) * nd, pipeline_mode=pl.Buffered(1))


def _rows(tm, width):
    return pl.BlockSpec((tm, width), lambda i: (i, 0))


def _params(sem, vmem=None):
    return pltpu.CompilerParams(dimension_semantics=sem, vmem_limit_bytes=vmem)


def _inproj_body(x_ref, gain_ref, wq_ref, wkt_ref, wvo_ref, wg_ref, q_ref, kt_ref, v_ref, so_ref, g_ref):
    xn = _rmsnorm(x_ref[...], gain_ref[...]).astype(BF)
    q_ref[...] = _dot(xn, wq_ref[...]).astype(BF)
    kt = lax.dot_general(wkt_ref[...], xn, (((1,), (1,)), ((), ())), preferred_element_type=F32)
    kt_ref[0] = (kt * (DK ** -0.5)).astype(BF)
    v_ref[...] = _dot(xn, wvo_ref[:, 0:V_WIDTH]).astype(BF)
    so_ref[...] = jax.nn.sigmoid(_dot(xn, wvo_ref[:, V_WIDTH:2 * V_WIDTH])).astype(BF)
    g_ref[...] = _dot(xn, wg_ref[...])[:, 0:N_GATES]


def _inproj(x2, gain, w_q, w_kt, w_vo, w_gate, tm, seq):
    m, d = x2.shape
    tiles_per_seq = seq // tm
    weights = [gain, w_q, w_kt, w_vo, w_gate]
    return pl.pallas_call(
        _inproj_body,
        grid=(m // tm,),
        in_specs=[_rows(tm, d)] + [_resident(w.shape) for w in weights],
        out_specs=[_rows(tm, QK_WIDTH),
                   pl.BlockSpec((1, QK_WIDTH, tm), lambda i: (i // tiles_per_seq, 0, i % tiles_per_seq)),
                   _rows(tm, V_WIDTH), _rows(tm, V_WIDTH), _rows(tm, N_GATES)],
        out_shape=[jax.ShapeDtypeStruct((m, QK_WIDTH), BF), jax.ShapeDtypeStruct((m // seq, QK_WIDTH, seq), BF),
                   jax.ShapeDtypeStruct((m, V_WIDTH), BF), jax.ShapeDtypeStruct((m, V_WIDTH), BF),
                   jax.ShapeDtypeStruct((m, N_GATES), F32)],
        compiler_params=_params(("parallel",), V7X_SCOPED_VMEM_BYTES),
        name="mlstm_inproj",
    )(x2, *weights)


def _log_sigmoid(x):
    return jnp.minimum(x, 0.0) - jnp.log1p(jnp.exp(-jnp.abs(x)))


def _split3(x):
    hi = x.astype(BF)
    r1 = x - hi.astype(F32)
    mid = r1.astype(BF)
    lo = (r1 - mid.astype(F32)).astype(BF)
    return hi, mid, lo


def _prep_body(gt_ref, bias_ref, out_ref, mprev_ref):
    nc = gt_ref.shape[1]
    L = CHUNK
    g = gt_ref[0] + bias_ref[...][None]
    i_pre = g[:, 0:8, :].reshape(nc * 8, L)
    logf = _log_sigmoid(g[:, 8:16, :]).reshape(nc * 8, L)

    row = lax.broadcasted_iota(jnp.int32, (nc * 8, L), 0)
    lane = lax.broadcasted_iota(jnp.int32, (nc * 8, L), 1)
    is_fwd = (row % 8) < HEADS

    s_idx = lax.broadcasted_iota(jnp.int32, (L, L), 0)
    t_idx = lax.broadcasted_iota(jnp.int32, (L, L), 1)
    tri_pre = (s_idx <= t_idx).astype(BF)
    tri_suf = (s_idx >= t_idx).astype(BF)
    pre = jnp.zeros((nc * 8, L), F32)
    suf = jnp.zeros((nc * 8, L), F32)
    for piece in _split3(logf):
        pre = pre + _dot(piece, tri_pre)
        suf = suf + _dot(piece, tri_suf)
    b = jnp.where(is_fwd, pre, suf)
    a = i_pre - b

    pm = a
    sm = a
    k = 1
    while k < L:
        pm = jnp.where(lane >= k, jnp.maximum(pm, pltpu.roll(pm, k, 1)), pm)
        sm = jnp.where(lane < L - k, jnp.maximum(sm, pltpu.roll(sm, L - k, 1)), sm)
        k *= 2
    cm = jnp.where(is_fwd, pm, sm)

    amax = jnp.broadcast_to(jnp.max(a, axis=1, keepdims=True), (nc * 8, L)).reshape(nc, 8, L)
    b_last = jnp.broadcast_to(jnp.sum(logf, axis=1, keepdims=True), (nc * 8, L)).reshape(nc, 8, L)

    row8 = lax.broadcasted_iota(jnp.int32, (8, L), 0) < HEADS
    m = jnp.zeros((8, L), F32)
    for c in range(nc):
        cb = nc - 1 - c
        mprev_ref[c, 0:HEADS, :] = m[0:HEADS]
        mprev_ref[cb, HEADS:8, :] = m[HEADS:8]
        am = jnp.where(row8, amax[c], amax[cb])
        bl = jnp.where(row8, b_last[c], b_last[cb])
        m = bl + jnp.maximum(m, am)

    mprev = mprev_ref[...]
    a3 = a.reshape(nc, 8, L)
    b3 = b.reshape(nc, 8, L)
    cm3 = cm.reshape(nc, 8, L)
    sigma = jnp.maximum(mprev, amax)
    mm = jnp.maximum(mprev, cm3)
    out_ref[0, 0] = a3 * LOG2E
    out_ref[0, 1] = mprev * LOG2E
    out_ref[0, 2] = mm * LOG2E
    out_ref[0, 3] = jnp.exp(-(b3 + mm))
    out_ref[0, 4] = jnp.exp(a3 - sigma)
    out_ref[0, 5] = jnp.exp(mprev - sigma)
    out_ref[0, 6] = jnp.zeros((nc, 8, L), F32)
    out_ref[0, 7] = jnp.zeros((nc, 8, L), F32)


def _prep(gt, bias_rows):
    bsz, nc, _, L = gt.shape
    return pl.pallas_call(
        _prep_body,
        grid=(bsz,),
        in_specs=[pl.BlockSpec((1, nc, N_GATES, L), lambda b: (b, 0, 0, 0)), _resident(bias_rows.shape)],
        out_specs=pl.BlockSpec((1, N_PREP, nc, 8, L), lambda b: (b, 0, 0, 0, 0)),
        out_shape=jax.ShapeDtypeStruct((bsz, N_PREP, nc, 8, L), F32),
        scratch_shapes=[pltpu.VMEM((nc, 8, L), F32)],
        compiler_params=_params(("parallel",)),
        name="mlstm_gate_prep",
    )(gt, bias_rows)


def _scan_body(qf_ref, ktf_ref, vf_ref, pf_ref, qb_ref, ktb_ref, vb_ref, pb_ref, hf_ref, hb_ref, st_ref):
    L = CHUNK
    cb = pf_ref.shape[2]

    @pl.when(pl.program_id(1) == 0)
    def _():
        st_ref[...] = jnp.zeros(st_ref.shape, F32)

    t_idx = lax.broadcasted_iota(jnp.int32, (L, 2 * L), 0)
    s_idx = lax.broadcasted_iota(jnp.int32, (L, 2 * L), 1)
    masks = ((s_idx <= t_idx) | (s_idx >= L), (s_idx >= t_idx))
    ones_blk = jnp.ones((L, L), BF)

    dirs = ((qf_ref, ktf_ref, vf_ref, pf_ref, hf_ref), (qb_ref, ktb_ref, vb_ref, pb_ref, hb_ref))

    def body(ci, carry):
        probs = []
        for d, (q_ref, kt_ref, v_ref, p_ref, h_ref) in enumerate(dirs):
            c = ci if d == 0 else cb - 1 - ci
            rows = pl.ds(pl.multiple_of(c * L, L), L)
            a2, mprev2, mm2, eneg, w, decay = [p_ref[0, qi, c] for qi in range(6)]
            mm2_cols = mm2.T
            eneg_cols = eneg.T
            for hd in range(HEADS):
                rix = d * HEADS + hd
                qc = q_ref[0, rows, hd * DK:(hd + 1) * DK]
                ktc = kt_ref[0, hd * DK:(hd + 1) * DK, rows]
                vc = v_ref[0, rows, hd * DV:(hd + 1) * DV]
                probs.append(dict(
                    d=d, hd=hd, rows=rows, h_ref=h_ref, qc=qc, ktc=ktc, s=_dot(qc, ktc),
                    top=jnp.concatenate([vc, ones_blk], axis=1),
                    rowvec=jnp.concatenate([a2[rix:rix + 1], mprev2[rix:rix + 1]], axis=1),
                    mm2=mm2_cols[:, rix:rix + 1], eneg=eneg_cols[:, rix:rix + 1],
                    w=w[rix:rix + 1], dec=decay[rix:rix + 1]))
        for pr in probs:
            dmat = jnp.where(masks[pr["d"]], jnp.exp2(pr["rowvec"] - pr["mm2"]), 0.0)
            lhs = (jnp.concatenate([pr["s"], pr["qc"].astype(F32)], axis=1) * dmat).astype(BF)
            st = st_ref[pr["d"], pr["hd"]]
            pr["main"] = _dot(lhs, jnp.concatenate([pr["top"], st.astype(BF)], axis=0))
            ktw = (pr["ktc"].astype(F32) * pr["w"]).astype(BF)
            dec = pr["dec"]
            pr["st_new"] = jnp.concatenate([dec, dec, dec], axis=1) * st + _dot(ktw, pr["top"])
        for pr in probs:
            main = pr["main"]
            r = 1.0 / jnp.maximum(jnp.abs(main[:, DV:]), pr["eneg"])
            h = main[:, 0:DV] * jnp.concatenate([r, r], axis=1)
            pr["h_ref"][0, pr["rows"], pr["hd"] * DV:(pr["hd"] + 1) * DV] = h.astype(BF)
            st_ref[pr["d"], pr["hd"]] = pr["st_new"]
        return carry

    lax.fori_loop(0, cb, body, 0)


def _scan(q, kt, v, prep, sb):
    bsz, s, _ = q.shape
    nb = s // sb
    cb = sb // CHUNK
    fwd3 = lambda b, j: (b, j, 0)
    bwd3 = lambda b, j: (b, nb - 1 - j, 0)
    in_specs = []
    for blk3, blkt, blkp in ((fwd3, lambda b, j: (b, 0, j), lambda b, j: (b, 0, j, 0, 0)),
                             (bwd3, lambda b, j: (b, 0, nb - 1 - j), lambda b, j: (b, 0, nb - 1 - j, 0, 0))):
        in_specs += [pl.BlockSpec((1, sb, QK_WIDTH), blk3), pl.BlockSpec((1, QK_WIDTH, sb), blkt),
                     pl.BlockSpec((1, sb, V_WIDTH), blk3), pl.BlockSpec((1, N_PREP, cb, 8, CHUNK), blkp)]
    return pl.pallas_call(
        _scan_body,
        grid=(bsz, nb),
        in_specs=in_specs,
        out_specs=[pl.BlockSpec((1, sb, V_WIDTH), fwd3), pl.BlockSpec((1, sb, V_WIDTH), bwd3)],
        out_shape=[jax.ShapeDtypeStruct((bsz, s, V_WIDTH), BF)] * 2,
        scratch_shapes=[pltpu.VMEM((2, HEADS, DK, DV + CHUNK), F32)],
        compiler_params=_params(("parallel", "arbitrary"), V7X_SCOPED_VMEM_BYTES),
        name="mlstm_scan",
    )(q, kt, v, prep, q, kt, v, prep)


def _mlp(xn, w1_ref, w2_ref, ff_chunk):
    d_ff = w1_ref.shape[1]
    acc = None
    for c0 in range(0, d_ff, ff_chunk):
        a = _dot(xn, w1_ref[:, c0:c0 + ff_chunk])
        a = jnp.square(jnp.maximum(a, 0.0)).astype(BF)
        part = _dot(a, w2_ref[c0:c0 + ff_chunk, :])
        acc = part if acc is None else acc + part
    return acc


def _mlp_and_embed(h, p_ref, nmlp_ref, w1_ref, w2_ref, nple_ref, gw_ref, gb_ref, plew_ref, ff_chunk):
    h = h + _mlp(_rmsnorm(h, nmlp_ref[...]).astype(BF), w1_ref, w2_ref, ff_chunk)
    gate = jax.nn.sigmoid(_dot(_rmsnorm(h, nple_ref[...]).astype(BF), gw_ref[...]) + gb_ref[...])
    return h + gate * _dot(p_ref[...].astype(BF), plew_ref[...])


def _tail0_body(hf_ref, hb_ref, so_ref, x_ref, p_ref, hnorm_ref, wout_ref, nmlp_ref, w1_ref, w2_ref, nple_ref,
                gw_ref, gb_ref, plew_ref, nmix_ref, pin_ref, h_ref, u_ref, *, ff_chunk):
    hs = hf_ref[...].astype(F32) + hb_ref[...].astype(F32)
    parts = []
    for hd in range(HEADS):
        blk = hs[:, hd * DV:(hd + 1) * DV]
        ms = jnp.mean(blk * blk, axis=-1, keepdims=True)
        parts.append(blk * lax.rsqrt(ms + EPS))
    hn = jnp.concatenate(parts, axis=1) * hnorm_ref[...] * so_ref[...].astype(F32)
    h = x_ref[...] + _dot(hn.astype(BF), wout_ref[...])
    h = _mlp_and_embed(h, p_ref, nmlp_ref, w1_ref, w2_ref, nple_ref, gw_ref, gb_ref, plew_ref, ff_chunk)
    h_ref[...] = h
    u_ref[...] = _dot(_rmsnorm(h, nmix_ref[...]).astype(BF), pin_ref[...])


def _tail0(hf, hb, so, x2, p2, weights, tm, ff_chunk):
    m, d = x2.shape
    return pl.pallas_call(
        functools.partial(_tail0_body, ff_chunk=ff_chunk),
        grid=(m // tm,),
        in_specs=[_rows(tm, d), _rows(tm, d), _rows(tm, d), _rows(tm, d), _rows(tm, p2.shape[1])]
                 + [_resident(w.shape) for w in weights],
        out_specs=[_rows(tm, d), _rows(tm, d)],
        out_shape=[jax.ShapeDtypeStruct((m, d), F32), jax.ShapeDtypeStruct((m, d), F32)],
        compiler_params=_params(("parallel",), V7X_SCOPED_VMEM_BYTES),
        name="layer0_tail",
    )(hf, hb, so, x2, p2, *weights)


def _layer1_body(u_ref, uprev_ref, unext_ref, h_ref, p_ref, wgrp_ref, scale_ref, pout_ref, nmlp_ref, w1_ref,
                 w2_ref, nple_ref, gw_ref, gb_ref, plew_ref, nfin_ref, out_ref, ext_ref, *, seq, ff_chunk):
    tm, d = u_ref.shape
    gw = d // len(POOL_WINDOWS)
    t0 = (pl.program_id(0) * tm) % seq
    ext_ref[0:POOL_HALO, :] = jnp.where(t0 > 0, uprev_ref[...], 0.0)
    ext_ref[POOL_HALO:POOL_HALO + tm, :] = u_ref[...]
    ext_ref[POOL_HALO + tm:, :] = jnp.where(t0 + tm < seq, unext_ref[...], 0.0)

    t = t0 + lax.broadcasted_iota(jnp.int32, (tm, gw), 0)
    mixed = []
    for gi, win in enumerate(POOL_WINDOWS):
        cols = slice(gi * gw, (gi + 1) * gw)
        lo_off = win // 2
        total = None
        for j in range(-lo_off, win - lo_off):
            sl = ext_ref[pl.ds(POOL_HALO + j, tm), cols]
            total = sl if total is None else total + sl
        cnt = jnp.minimum(t + (win - lo_off), seq) - jnp.maximum(t - lo_off, 0)
        y = total / cnt.astype(F32) - u_ref[:, cols]
        mixed.append(_dot(y.astype(BF), wgrp_ref[gi]))
    y = jnp.concatenate(mixed, axis=1) * scale_ref[...]
    h = h_ref[...] + _dot(y.astype(BF), pout_ref[...])
    h = _mlp_and_embed(h, p_ref, nmlp_ref, w1_ref, w2_ref, nple_ref, gw_ref, gb_ref, plew_ref, ff_chunk)
    out_ref[...] = _rmsnorm(h, nfin_ref[...])


def _layer1(u, h, p2, weights, tm, seq, ff_chunk):
    m, d = u.shape
    halo_blocks_per_tile = tm // POOL_HALO
    last_halo_block = m // POOL_HALO - 1
    return pl.pallas_call(
        functools.partial(_layer1_body, seq=seq, ff_chunk=ff_chunk),
        grid=(m // tm,),
        in_specs=[_rows(tm, d),
                  pl.BlockSpec((POOL_HALO, d), lambda i: (jnp.maximum(i * halo_blocks_per_tile - 1, 0), 0)),
                  pl.BlockSpec((POOL_HALO, d),
                               lambda i: (jnp.minimum((i + 1) * halo_blocks_per_tile, last_halo_block), 0)),
                  _rows(tm, d), _rows(tm, p2.shape[1])]
                 + [_resident(w.shape) for w in weights],
        out_specs=_rows(tm, d),
        out_shape=jax.ShapeDtypeStruct((m, d), F32),
        scratch_shapes=[pltpu.VMEM((tm + 2 * POOL_HALO, d), F32)],
        compiler_params=_params(("parallel",), V7X_SCOPED_VMEM_BYTES),
        name="layer1_pool_mlp",
    )(u, u, u, h, p2, *weights)


def kernel(x, p, norm_mix, norm_mlp, norm_ple, norm_final, mlstm_w_in, mlstm_b_gates, mlstm_head_norm,
           mlstm_w_out, pool_w_in, pool_w_grp, pool_scale, pool_w_out, mlp_w1, mlp_w2, ple_w, ple_gate_w,
           ple_gate_b):
    bsz, seq, d = x.shape
    m = bsz * seq
    nc = seq // CHUNK
    tm_proj, tm_fused, ff_chunk, scan_block = min(512, seq), min(512, seq), 1024, min(1024, seq)
    assert seq % scan_block == 0 and seq % tm_fused == 0 and seq % tm_proj == 0 and scan_block % CHUNK == 0
    row = lambda v: v.reshape(1, -1)
    bf = lambda w: w.astype(BF)

    x2 = x.reshape(m, d)
    w_in = mlstm_w_in[0]
    n_main = 2 * QK_WIDTH + 2 * V_WIDTH
    gate_perm = jnp.array([0, 1, 2, 3, 8, 9, 10, 11, 4, 5, 6, 7, 12, 13, 14, 15], jnp.int32)
    w_gate = jnp.pad(w_in[:, n_main:][:, gate_perm], ((0, 0), (0, LANES - N_GATES)))
    bias_rows = jnp.broadcast_to(mlstm_b_gates[0].reshape(N_GATES)[gate_perm][:, None], (N_GATES, CHUNK))

    q, kt, v, so, g = _inproj(x2, row(norm_mix[0]), bf(w_in[:, 0:QK_WIDTH]), bf(w_in[:, QK_WIDTH:2 * QK_WIDTH].T),
                              bf(w_in[:, 2 * QK_WIDTH:n_main]), bf(w_gate), tm_proj, seq)

    gt = g.reshape(bsz, nc, CHUNK, N_GATES).transpose(0, 1, 3, 2)
    prep = _prep(gt, bias_rows)

    hf, hb = _scan(q.reshape(bsz, seq, QK_WIDTH), kt, v.reshape(bsz, seq, V_WIDTH), prep, scan_block)

    tail_weights = [row(mlstm_head_norm[0]), bf(mlstm_w_out[0]), row(norm_mlp[0]), bf(mlp_w1[0]),
                    bf(mlp_w2[0]), row(norm_ple[0]), bf(ple_gate_w[0]), row(ple_gate_b[0]), bf(ple_w[0]),
                    row(norm_mix[1]), bf(pool_w_in[0])]
    h, u = _tail0(hf.reshape(m, V_WIDTH), hb.reshape(m, V_WIDTH), so, x2, p[0].reshape(m, -1), tail_weights,
                  tm_fused, ff_chunk)

    l1_weights = [bf(pool_w_grp[0]), row(pool_scale[0]), bf(pool_w_out[0]), row(norm_mlp[1]), bf(mlp_w1[1]),
                  bf(mlp_w2[1]), row(norm_ple[1]), bf(ple_gate_w[1]), row(ple_gate_b[1]), bf(ple_w[1]),
                  row(norm_final)]
    out = _layer1(u, h, p[1].reshape(m, -1), l1_weights, tm_fused, seq, ff_chunk)
    return out.reshape(bsz, seq, d)
```

```python
import functools

import jax
import jax.numpy as jnp
from jax import lax
from jax.experimental import pallas as pl
from jax.experimental.pallas import tpu as pltpu

EPS = 1e-6
HEADS = 4
DK = 128
DV = 256
CHUNK = 128
QK_WIDTH = HEADS * DK
V_WIDTH = HEADS * DV
N_GATES = 4 * HEADS
POOL_WINDOWS = (2, 4, 8, 16)
POOL_HALO = 8
N_PREP = 8

LANES = 128
SUBLANES = 8
V7X_SCOPED_VMEM_BYTES = 60000 * 1024

LOG2E = 1.4426950408889634

BF = jnp.bfloat16
F32 = jnp.float32


def _dot(a, b):
    return jnp.dot(a, b, preferred_element_type=F32)


def _rmsnorm(x, gain):
    ms = jnp.mean(x * x, axis=-1, keepdims=True)
    return x * lax.rsqrt(ms + EPS) * gain


def _resident(shape):
    nd = len(shape)
    return pl.BlockSpec(shape, lambda *_: (0,) * nd, pipeline_mode=pl.Buffered(1))


def _layer_of(stacked, layer):
    nd = stacked.ndim
    return pl.BlockSpec((1,) + stacked.shape[1:], lambda *_: (layer,) + (0,) * (nd - 1),
                        pipeline_mode=pl.Buffered(1))


def _rows(tm, width, first_tile=0):
    return pl.BlockSpec((tm, width), lambda i: (i + first_tile, 0))


def _params(sem, vmem=None):
    return pltpu.CompilerParams(dimension_semantics=sem, vmem_limit_bytes=vmem)


def _inproj_body(x_ref, gain_ref, w_ref, wkt_ref, wgt_ref, q_ref, kt_ref, v_ref, so_ref, gt_ref):
    v0 = 2 * QK_WIDTH
    nt = (((1,), (1,)), ((), ()))
    xn = _rmsnorm(x_ref[...], gain_ref[...]).astype(BF)
    q_ref[...] = _dot(xn, w_ref[0, :, 0:QK_WIDTH]).astype(BF)
    kt = lax.dot_general(wkt_ref[...], xn, nt, preferred_element_type=F32)
    kt_ref[0] = (kt * (DK ** -0.5)).astype(BF)
    v_ref[...] = _dot(xn, w_ref[0, :, v0:v0 + V_WIDTH]).astype(BF)
    so_ref[...] = jax.nn.sigmoid(_dot(xn, w_ref[0, :, v0 + V_WIDTH:v0 + 2 * V_WIDTH])).astype(BF)
    gt = lax.dot_general(wgt_ref[...], xn, nt, preferred_element_type=F32)
    for c in range(gt_ref.shape[1]):
        gt_ref[0, c] = gt[:, c * CHUNK:(c + 1) * CHUNK]


def _inproj(x2, gain, w_in, w_kt, w_gt, tm, seq):
    m, d = x2.shape
    tiles_per_seq = seq // tm
    cpt = tm // CHUNK
    per_seq = lambda i: (i // tiles_per_seq, i % tiles_per_seq)
    return pl.pallas_call(
        _inproj_body,
        grid=(m // tm,),
        in_specs=[_rows(tm, d), _resident(gain.shape), _layer_of(w_in, 0), _resident(w_kt.shape),
                  _resident(w_gt.shape)],
        out_specs=[_rows(tm, QK_WIDTH),
                   pl.BlockSpec((1, QK_WIDTH, tm), lambda i: (per_seq(i)[0], 0, per_seq(i)[1])),
                   _rows(tm, V_WIDTH), _rows(tm, V_WIDTH),
                   pl.BlockSpec((1, cpt, N_GATES, CHUNK), lambda i: (per_seq(i)[0], per_seq(i)[1], 0, 0))],
        out_shape=[jax.ShapeDtypeStruct((m, QK_WIDTH), BF), jax.ShapeDtypeStruct((m // seq, QK_WIDTH, seq), BF),
                   jax.ShapeDtypeStruct((m, V_WIDTH), BF), jax.ShapeDtypeStruct((m, V_WIDTH), BF),
                   jax.ShapeDtypeStruct((m // seq, seq // CHUNK, N_GATES, CHUNK), F32)],
        compiler_params=_params(("parallel",), V7X_SCOPED_VMEM_BYTES),
        name="mlstm_inproj",
    )(x2, gain, w_in, w_kt, w_gt)


def _log_sigmoid(x):
    return jnp.minimum(x, 0.0) - jnp.log1p(jnp.exp(-jnp.abs(x)))


def _split3(x):
    hi = x.astype(BF)
    r1 = x - hi.astype(F32)
    mid = r1.astype(BF)
    lo = (r1 - mid.astype(F32)).astype(BF)
    return hi, mid, lo


def _prep_body(gt_ref, bias_ref, out_ref, mprev_ref):
    nc = gt_ref.shape[1]
    L = CHUNK
    g = gt_ref[0] + bias_ref[...][None]
    i_pre = g[:, 0:8, :].reshape(nc * 8, L)
    logf = _log_sigmoid(g[:, 8:16, :]).reshape(nc * 8, L)

    row = lax.broadcasted_iota(jnp.int32, (nc * 8, L), 0)
    lane = lax.broadcasted_iota(jnp.int32, (nc * 8, L), 1)
    is_fwd = (row % 8) < HEADS

    s_idx = lax.broadcasted_iota(jnp.int32, (L, L), 0)
    t_idx = lax.broadcasted_iota(jnp.int32, (L, L), 1)
    tri_pre = (s_idx <= t_idx).astype(BF)
    tri_suf = (s_idx >= t_idx).astype(BF)
    pre = jnp.zeros((nc * 8, L), F32)
    suf = jnp.zeros((nc * 8, L), F32)
    for piece in _split3(logf):
        pre = pre + _dot(piece, tri_pre)
        suf = suf + _dot(piece, tri_suf)
    b = jnp.where(is_fwd, pre, suf)
    a = i_pre - b

    pm = a
    sm = a
    k = 1
    while k < L:
        pm = jnp.where(lane >= k, jnp.maximum(pm, pltpu.roll(pm, k, 1)), pm)
        sm = jnp.where(lane < L - k, jnp.maximum(sm, pltpu.roll(sm, L - k, 1)), sm)
        k *= 2
    cm = jnp.where(is_fwd, pm, sm)

    amax = jnp.broadcast_to(jnp.max(a, axis=1, keepdims=True), (nc * 8, L)).reshape(nc, 8, L)
    b_last = jnp.broadcast_to(jnp.sum(logf, axis=1, keepdims=True), (nc * 8, L)).reshape(nc, 8, L)

    row8 = lax.broadcasted_iota(jnp.int32, (8, L), 0) < HEADS
    m = jnp.zeros((8, L), F32)
    for c in range(nc):
        cb = nc - 1 - c
        mprev_ref[c, 0:HEADS, :] = m[0:HEADS]
        mprev_ref[cb, HEADS:8, :] = m[HEADS:8]
        am = jnp.where(row8, amax[c], amax[cb])
        bl = jnp.where(row8, b_last[c], b_last[cb])
        m = bl + jnp.maximum(m, am)

    mprev = mprev_ref[...]
    a3 = a.reshape(nc, 8, L)
    b3 = b.reshape(nc, 8, L)
    cm3 = cm.reshape(nc, 8, L)
    sigma = jnp.maximum(mprev, amax)
    mm = jnp.maximum(mprev, cm3)
    out_ref[0, 0] = a3 * LOG2E
    out_ref[0, 1] = mprev * LOG2E
    out_ref[0, 2] = mm * LOG2E
    out_ref[0, 3] = jnp.exp(-(b3 + mm))
    out_ref[0, 4] = jnp.exp(a3 - sigma)
    out_ref[0, 5] = jnp.exp(mprev - sigma)
    out_ref[0, 6] = jnp.zeros((nc, 8, L), F32)
    out_ref[0, 7] = jnp.zeros((nc, 8, L), F32)


def _prep(gt, bias_rows):
    bsz, nc, _, L = gt.shape
    return pl.pallas_call(
        _prep_body,
        grid=(bsz,),
        in_specs=[pl.BlockSpec((1, nc, N_GATES, L), lambda b: (b, 0, 0, 0)), _resident(bias_rows.shape)],
        out_specs=pl.BlockSpec((1, N_PREP, nc, 8, L), lambda b: (b, 0, 0, 0, 0)),
        out_shape=jax.ShapeDtypeStruct((bsz, N_PREP, nc, 8, L), F32),
        scratch_shapes=[pltpu.VMEM((nc, 8, L), F32)],
        compiler_params=_params(("parallel",)),
        name="mlstm_gate_prep",
    )(gt, bias_rows)


def _scan_body(qf_ref, ktf_ref, vf_ref, pf_ref, qb_ref, ktb_ref, vb_ref, pb_ref, hf_ref, hb_ref, st_ref):
    L = CHUNK
    cb = pf_ref.shape[2]

    @pl.when(pl.program_id(1) == 0)
    def _():
        st_ref[...] = jnp.zeros(st_ref.shape, F32)

    t_idx = lax.broadcasted_iota(jnp.int32, (L, 2 * L), 0)
    s_idx = lax.broadcasted_iota(jnp.int32, (L, 2 * L), 1)
    masks = ((s_idx <= t_idx) | (s_idx >= L), (s_idx >= t_idx))
    ones_blk = jnp.ones((L, L), BF)

    dirs = ((qf_ref, ktf_ref, vf_ref, pf_ref, hf_ref), (qb_ref, ktb_ref, vb_ref, pb_ref, hb_ref))

    def body(ci, carry):
        probs = []
        for d, (q_ref, kt_ref, v_ref, p_ref, h_ref) in enumerate(dirs):
            c = ci if d == 0 else cb - 1 - ci
            rows = pl.ds(pl.multiple_of(c * L, L), L)
            a2, mprev2, mm2, eneg, w, decay = [p_ref[0, qi, c] for qi in range(6)]
            mm2_cols = mm2.T
            eneg_cols = eneg.T
            for hd in range(HEADS):
                rix = d * HEADS + hd
                qc = q_ref[0, rows, hd * DK:(hd + 1) * DK]
                ktc = kt_ref[0, hd * DK:(hd + 1) * DK, rows]
                vc = v_ref[0, rows, hd * DV:(hd + 1) * DV]
                probs.append(dict(
                    d=d, hd=hd, rows=rows, h_ref=h_ref, qc=qc, ktc=ktc, s=_dot(qc, ktc),
                    top=jnp.concatenate([vc, ones_blk], axis=1),
                    rowvec=jnp.concatenate([a2[rix:rix + 1], mprev2[rix:rix + 1]], axis=1),
                    mm2=mm2_cols[:, rix:rix + 1], eneg=eneg_cols[:, rix:rix + 1],
                    w=w[rix:rix + 1], dec=decay[rix:rix + 1]))
        for pr in probs:
            dmat = jnp.where(masks[pr["d"]], jnp.exp2(pr["rowvec"] - pr["mm2"]), 0.0)
            lhs = (jnp.concatenate([pr["s"], pr["qc"].astype(F32)], axis=1) * dmat).astype(BF)
            st = st_ref[pr["d"], pr["hd"]]
            pr["main"] = _dot(lhs, jnp.concatenate([pr["top"], st.astype(BF)], axis=0))
            ktw = (pr["ktc"].astype(F32) * pr["w"]).astype(BF)
            dec = pr["dec"]
            pr["st_new"] = jnp.concatenate([dec, dec, dec], axis=1) * st + _dot(ktw, pr["top"])
        for pr in probs:
            main = pr["main"]
            r = 1.0 / jnp.maximum(jnp.abs(main[:, DV:]), pr["eneg"])
            h = main[:, 0:DV] * jnp.concatenate([r, r], axis=1)
            pr["h_ref"][0, pr["rows"], pr["hd"] * DV:(pr["hd"] + 1) * DV] = h.astype(BF)
            st_ref[pr["d"], pr["hd"]] = pr["st_new"]
        return carry

    lax.fori_loop(0, cb, body, 0)


def _scan(q, kt, v, prep, sb):
    bsz, s, _ = q.shape
    nb = s // sb
    cb = sb // CHUNK
    fwd3 = lambda b, j: (b, j, 0)
    bwd3 = lambda b, j: (b, nb - 1 - j, 0)
    in_specs = []
    for blk3, blkt, blkp in ((fwd3, lambda b, j: (b, 0, j), lambda b, j: (b, 0, j, 0, 0)),
                             (bwd3, lambda b, j: (b, 0, nb - 1 - j), lambda b, j: (b, 0, nb - 1 - j, 0, 0))):
        in_specs += [pl.BlockSpec((1, sb, QK_WIDTH), blk3), pl.BlockSpec((1, QK_WIDTH, sb), blkt),
                     pl.BlockSpec((1, sb, V_WIDTH), blk3), pl.BlockSpec((1, N_PREP, cb, 8, CHUNK), blkp)]
    return pl.pallas_call(
        _scan_body,
        grid=(bsz, nb),
        in_specs=in_specs,
        out_specs=[pl.BlockSpec((1, sb, V_WIDTH), fwd3), pl.BlockSpec((1, sb, V_WIDTH), bwd3)],
        out_shape=[jax.ShapeDtypeStruct((bsz, s, V_WIDTH), BF)] * 2,
        scratch_shapes=[pltpu.VMEM((2, HEADS, DK, DV + CHUNK), F32)],
        compiler_params=_params(("parallel", "arbitrary"), V7X_SCOPED_VMEM_BYTES),
        name="mlstm_scan",
    )(q, kt, v, prep, q, kt, v, prep)


def _mlp(xn, w1_ref, w2_ref, ff_chunk):
    d_ff = w1_ref.shape[2]
    acc = None
    for c0 in range(0, d_ff, ff_chunk):
        a = _dot(xn, w1_ref[0, :, c0:c0 + ff_chunk])
        a = jnp.square(jnp.maximum(a, 0.0)).astype(BF)
        part = _dot(a, w2_ref[0, c0:c0 + ff_chunk, :])
        acc = part if acc is None else acc + part
    return acc


def _mlp_and_embed(h, p_ref, nmlp_ref, w1_ref, w2_ref, nple_ref, gw_ref, gb_ref, plew_ref, ff_chunk):
    h = h + _mlp(_rmsnorm(h, nmlp_ref[...]).astype(BF), w1_ref, w2_ref, ff_chunk)
    gate = jax.nn.sigmoid(_dot(_rmsnorm(h, nple_ref[...]).astype(BF), gw_ref[0]) + gb_ref[...])
    return h + gate * _dot(p_ref[...].astype(BF), plew_ref[0])


def _tail0_body(hf_ref, hb_ref, so_ref, x_ref, p_ref, hnorm_ref, wout_ref, nmlp_ref, w1_ref, w2_ref, nple_ref,
                gw_ref, gb_ref, plew_ref, nmix_ref, pin_ref, h_ref, u_ref, *, ff_chunk):
    hs = hf_ref[...].astype(F32) + hb_ref[...].astype(F32)
    parts = []
    for hd in range(HEADS):
        blk = hs[:, hd * DV:(hd + 1) * DV]
        ms = jnp.mean(blk * blk, axis=-1, keepdims=True)
        parts.append(blk * lax.rsqrt(ms + EPS))
    hn = jnp.concatenate(parts, axis=1) * hnorm_ref[...] * so_ref[...].astype(F32)
    h = x_ref[...] + _dot(hn.astype(BF), wout_ref[...])
    h = _mlp_and_embed(h, p_ref, nmlp_ref, w1_ref, w2_ref, nple_ref, gw_ref, gb_ref, plew_ref, ff_chunk)
    h_ref[...] = h
    u_ref[...] = _dot(_rmsnorm(h, nmix_ref[...]).astype(BF), pin_ref[...])


def _tail0(hf, hb, so, x2, p2, weights, tm, ff_chunk):
    m, d = x2.shape
    return pl.pallas_call(
        functools.partial(_tail0_body, ff_chunk=ff_chunk),
        grid=(m // tm,),
        in_specs=[_rows(tm, d), _rows(tm, d), _rows(tm, d), _rows(tm, d), _rows(tm, p2.shape[1])]
                 + [spec for _, spec in weights],
        out_specs=[_rows(tm, d), _rows(tm, d)],
        out_shape=[jax.ShapeDtypeStruct((m, d), F32), jax.ShapeDtypeStruct((m, d), F32)],
        compiler_params=_params(("parallel",), V7X_SCOPED_VMEM_BYTES),
        name="layer0_tail",
    )(hf, hb, so, x2, p2, *[w for w, _ in weights])


def _layer1_body(u_ref, uprev_ref, unext_ref, h_ref, p_ref, wgrp_ref, scale_ref, pout_ref, nmlp_ref, w1_ref,
                 w2_ref, nple_ref, gw_ref, gb_ref, plew_ref, nfin_ref, out_ref, ext_ref, *, seq, ff_chunk):
    tm, d = u_ref.shape
    gw = d // len(POOL_WINDOWS)
    t0 = (pl.program_id(0) * tm) % seq
    ext_ref[0:POOL_HALO, :] = jnp.where(t0 > 0, uprev_ref[...], 0.0)
    ext_ref[POOL_HALO:POOL_HALO + tm, :] = u_ref[...]
    ext_ref[POOL_HALO + tm:, :] = jnp.where(t0 + tm < seq, unext_ref[...], 0.0)

    t = t0 + lax.broadcasted_iota(jnp.int32, (tm, gw), 0)
    mixed = []
    for gi, win in enumerate(POOL_WINDOWS):
        cols = slice(gi * gw, (gi + 1) * gw)
        lo_off = win // 2
        total = None
        for j in range(-lo_off, win - lo_off):
            sl = ext_ref[pl.ds(POOL_HALO + j, tm), cols]
            total = sl if total is None else total + sl
        cnt = jnp.minimum(t + (win - lo_off), seq) - jnp.maximum(t - lo_off, 0)
        y = total / cnt.astype(F32) - u_ref[:, cols]
        mixed.append(_dot(y.astype(BF), wgrp_ref[gi]))
    y = jnp.concatenate(mixed, axis=1) * scale_ref[...]
    h = h_ref[...] + _dot(y.astype(BF), pout_ref[...])
    h = _mlp_and_embed(h, p_ref, nmlp_ref, w1_ref, w2_ref, nple_ref, gw_ref, gb_ref, plew_ref, ff_chunk)
    out_ref[...] = _rmsnorm(h, nfin_ref[...])


def _layer1(u, h, p2, weights, tm, seq, ff_chunk):
    m, d = u.shape
    halo_blocks_per_tile = tm // POOL_HALO
    last_halo_block = m // POOL_HALO - 1
    return pl.pallas_call(
        functools.partial(_layer1_body, seq=seq, ff_chunk=ff_chunk),
        grid=(m // tm,),
        in_specs=[_rows(tm, d),
                  pl.BlockSpec((POOL_HALO, d), lambda i: (jnp.maximum(i * halo_blocks_per_tile - 1, 0), 0)),
                  pl.BlockSpec((POOL_HALO, d),
                               lambda i: (jnp.minimum((i + 1) * halo_blocks_per_tile, last_halo_block), 0)),
                  _rows(tm, d), _rows(tm, p2.shape[1], first_tile=m // tm)]
                 + [spec for _, spec in weights],
        out_specs=_rows(tm, d),
        out_shape=jax.ShapeDtypeStruct((m, d), F32),
        scratch_shapes=[pltpu.VMEM((tm + 2 * POOL_HALO, d), F32)],
        compiler_params=_params(("parallel",), V7X_SCOPED_VMEM_BYTES),
        name="layer1_pool_mlp",
    )(u, u, u, h, p2, *[w for w, _ in weights])


def kernel(x, p, norm_mix, norm_mlp, norm_ple, norm_final, mlstm_w_in, mlstm_b_gates, mlstm_head_norm,
           mlstm_w_out, pool_w_in, pool_w_grp, pool_scale, pool_w_out, mlp_w1, mlp_w2, ple_w, ple_gate_w,
           ple_gate_b):
    bsz, seq, d = x.shape
    m = bsz * seq
    nc = seq // CHUNK
    tm_proj, tm_fused, ff_chunk, scan_block = min(512, seq), min(512, seq), 1024, min(1024, seq)
    assert seq % scan_block == 0 and seq % tm_fused == 0 and seq % tm_proj == 0 and scan_block % CHUNK == 0
    row = lambda v: v.reshape(1, -1)
    bf = lambda w: w.astype(BF)

    def whole(v):
        return v, _resident(v.shape)

    def layer(stacked, i):
        return stacked, _layer_of(stacked, i)

    x2 = x.reshape(m, d)
    p2 = p.reshape(-1, p.shape[-1])
    n_main = 2 * QK_WIDTH + 2 * V_WIDTH
    w_in = bf(mlstm_w_in)
    w_kt = w_in[0, :, QK_WIDTH:2 * QK_WIDTH].T
    gate_perm = jnp.array([0, 1, 2, 3, 8, 9, 10, 11, 4, 5, 6, 7, 12, 13, 14, 15], jnp.int32)
    w_gt = w_in[0, :, n_main:].T[gate_perm]
    bias_rows = jnp.broadcast_to(mlstm_b_gates[0].reshape(N_GATES)[gate_perm][:, None], (N_GATES, CHUNK))

    q, kt, v, so, gt = _inproj(x2, row(norm_mix[0]), w_in, w_kt, w_gt, tm_proj, seq)
    prep = _prep(gt, bias_rows)
    hf, hb = _scan(q.reshape(bsz, seq, QK_WIDTH), kt, v.reshape(bsz, seq, V_WIDTH), prep, scan_block)

    w1, w2, gate_w, emb_w = bf(mlp_w1), bf(mlp_w2), bf(ple_gate_w), bf(ple_w)

    def mlp_embed_weights(i):
        return [whole(row(norm_mlp[i])), layer(w1, i), layer(w2, i), whole(row(norm_ple[i])), layer(gate_w, i),
                whole(row(ple_gate_b[i])), layer(emb_w, i)]

    tail_weights = ([whole(row(mlstm_head_norm[0])), whole(bf(mlstm_w_out[0]))] + mlp_embed_weights(0)
                    + [whole(row(norm_mix[1])), whole(bf(pool_w_in[0]))])
    h, u = _tail0(hf.reshape(m, V_WIDTH), hb.reshape(m, V_WIDTH), so, x2, p2, tail_weights, tm_fused, ff_chunk)

    l1_weights = ([whole(bf(pool_w_grp[0])), whole(row(pool_scale[0])), whole(bf(pool_w_out[0]))]
                  + mlp_embed_weights(1) + [whole(row(norm_final))])
    out = _layer1(u, h, p2, l1_weights, tm_fused, seq, ff_chunk)
    return out.reshape(bsz, seq, d)
```

```python
import functools

import jax
import jax.numpy as jnp
from jax import lax
from jax.experimental import pallas as pl
from jax.experimental.pallas import tpu as pltpu

EPS = 1e-6
HEADS = 4
DK = 128
DV = 256
CHUNK = 128
QK_WIDTH = HEADS * DK
V_WIDTH = HEADS * DV
N_GATES = 4 * HEADS
POOL_WINDOWS = (2, 4, 8, 16)
POOL_HALO = 8
N_PREP = 8

LANES = 128
SUBLANES = 8
V7X_SCOPED_VMEM_BYTES = 60000 * 1024

LOG2E = 1.4426950408889634

BF = jnp.bfloat16
F32 = jnp.float32


def _dot(a, b):
    return jnp.dot(a, b, preferred_element_type=F32)


def _rmsnorm(x, gain):
    ms = jnp.mean(x * x, axis=-1, keepdims=True)
    return x * lax.rsqrt(ms + EPS) * gain


def _resident(shape):
    nd = len(shape)
    return pl.BlockSpec(shape, lambda *_: (0,) * nd, pipeline_mode=pl.Buffered(1))


def _layer_of(stacked, layer):
    nd = stacked.ndim
    return pl.BlockSpec((1,) + stacked.shape[1:], lambda *_: (layer,) + (0,) * (nd - 1),
                        pipeline_mode=pl.Buffered(1))


def _rows(tm, width, first_tile=0):
    return pl.BlockSpec((tm, width), lambda i: (i + first_tile, 0))


def _params(sem, vmem=None):
    return pltpu.CompilerParams(dimension_semantics=sem, vmem_limit_bytes=vmem)


def _inproj_body(x_ref, gain_ref, w_ref, wkt_ref, wgt_ref, q_ref, kt_ref, v_ref, so_ref, gt_ref):
    v0 = 2 * QK_WIDTH
    nt = (((1,), (1,)), ((), ()))
    xn = _rmsnorm(x_ref[...], gain_ref[...]).astype(BF)
    q_ref[...] = _dot(xn, w_ref[0, :, 0:QK_WIDTH]).astype(BF)
    kt = lax.dot_general(wkt_ref[...], xn, nt, preferred_element_type=F32)
    kt_ref[0] = (kt * (DK ** -0.5)).astype(BF)
    v_ref[...] = _dot(xn, w_ref[0, :, v0:v0 + V_WIDTH]).astype(BF)
    so_ref[...] = jax.nn.sigmoid(_dot(xn, w_ref[0, :, v0 + V_WIDTH:v0 + 2 * V_WIDTH])).astype(BF)
    gt = lax.dot_general(wgt_ref[...], xn, nt, preferred_element_type=F32)
    for c in range(gt_ref.shape[1]):
        gt_ref[0, c] = gt[:, c * CHUNK:(c + 1) * CHUNK]


def _inproj(x2, gain, w_in, w_kt, w_gt, tm, seq):
    m, d = x2.shape
    tiles_per_seq = seq // tm
    cpt = tm // CHUNK
    per_seq = lambda i: (i // tiles_per_seq, i % tiles_per_seq)
    return pl.pallas_call(
        _inproj_body,
        grid=(m // tm,),
        in_specs=[_rows(tm, d), _resident(gain.shape), _layer_of(w_in, 0), _resident(w_kt.shape),
                  _resident(w_gt.shape)],
        out_specs=[_rows(tm, QK_WIDTH),
                   pl.BlockSpec((1, QK_WIDTH, tm), lambda i: (per_seq(i)[0], 0, per_seq(i)[1])),
                   _rows(tm, V_WIDTH), _rows(tm, V_WIDTH),
                   pl.BlockSpec((1, cpt, N_GATES, CHUNK), lambda i: (per_seq(i)[0], per_seq(i)[1], 0, 0))],
        out_shape=[jax.ShapeDtypeStruct((m, QK_WIDTH), BF), jax.ShapeDtypeStruct((m // seq, QK_WIDTH, seq), BF),
                   jax.ShapeDtypeStruct((m, V_WIDTH), BF), jax.ShapeDtypeStruct((m, V_WIDTH), BF),
                   jax.ShapeDtypeStruct((m // seq, seq // CHUNK, N_GATES, CHUNK), F32)],
        compiler_params=_params(("parallel",), V7X_SCOPED_VMEM_BYTES),
        name="mlstm_inproj",
    )(x2, gain, w_in, w_kt, w_gt)


def _log_sigmoid(x):
    return jnp.minimum(x, 0.0) - jnp.log1p(jnp.exp(-jnp.abs(x)))


def _split3(x):
    hi = x.astype(BF)
    r1 = x - hi.astype(F32)
    mid = r1.astype(BF)
    lo = (r1 - mid.astype(F32)).astype(BF)
    return hi, mid, lo


def _prep_body(gt_ref, bias_ref, out_ref, mprev_ref):
    nc = gt_ref.shape[1]
    L = CHUNK
    g = gt_ref[0] + bias_ref[...][None]
    i_pre = g[:, 0:8, :].reshape(nc * 8, L)
    logf = _log_sigmoid(g[:, 8:16, :]).reshape(nc * 8, L)

    row = lax.broadcasted_iota(jnp.int32, (nc * 8, L), 0)
    lane = lax.broadcasted_iota(jnp.int32, (nc * 8, L), 1)
    is_fwd = (row % 8) < HEADS

    s_idx = lax.broadcasted_iota(jnp.int32, (L, L), 0)
    t_idx = lax.broadcasted_iota(jnp.int32, (L, L), 1)
    tri_pre = (s_idx <= t_idx).astype(BF)
    tri_suf = (s_idx >= t_idx).astype(BF)
    pre = jnp.zeros((nc * 8, L), F32)
    suf = jnp.zeros((nc * 8, L), F32)
    for piece in _split3(logf):
        pre = pre + _dot(piece, tri_pre)
        suf = suf + _dot(piece, tri_suf)
    b = jnp.where(is_fwd, pre, suf)
    a = i_pre - b

    pm = a
    sm = a
    k = 1
    while k < L:
        pm = jnp.where(lane >= k, jnp.maximum(pm, pltpu.roll(pm, k, 1)), pm)
        sm = jnp.where(lane < L - k, jnp.maximum(sm, pltpu.roll(sm, L - k, 1)), sm)
        k *= 2
    cm = jnp.where(is_fwd, pm, sm)

    amax = jnp.broadcast_to(jnp.max(a, axis=1, keepdims=True), (nc * 8, L)).reshape(nc, 8, L)
    b_last = jnp.broadcast_to(jnp.sum(logf, axis=1, keepdims=True), (nc * 8, L)).reshape(nc, 8, L)

    row8 = lax.broadcasted_iota(jnp.int32, (8, L), 0) < HEADS
    m = jnp.zeros((8, L), F32)
    for c in range(nc):
        cb = nc - 1 - c
        mprev_ref[c, 0:HEADS, :] = m[0:HEADS]
        mprev_ref[cb, HEADS:8, :] = m[HEADS:8]
        am = jnp.where(row8, amax[c], amax[cb])
        bl = jnp.where(row8, b_last[c], b_last[cb])
        m = bl + jnp.maximum(m, am)

    mprev = mprev_ref[...]
    a3 = a.reshape(nc, 8, L)
    b3 = b.reshape(nc, 8, L)
    cm3 = cm.reshape(nc, 8, L)
    sigma = jnp.maximum(mprev, amax)
    mm = jnp.maximum(mprev, cm3)
    out_ref[0, 0] = a3 * LOG2E
    out_ref[0, 1] = mprev * LOG2E
    out_ref[0, 2] = mm * LOG2E
    out_ref[0, 3] = jnp.exp(-(b3 + mm))
    out_ref[0, 4] = jnp.exp(a3 - sigma)
    out_ref[0, 5] = jnp.exp(mprev - sigma)
    out_ref[0, 6] = jnp.zeros((nc, 8, L), F32)
    out_ref[0, 7] = jnp.zeros((nc, 8, L), F32)


def _prep(gt, bias_rows):
    bsz, nc, _, L = gt.shape
    return pl.pallas_call(
        _prep_body,
        grid=(bsz,),
        in_specs=[pl.BlockSpec((1, nc, N_GATES, L), lambda b: (b, 0, 0, 0)), _resident(bias_rows.shape)],
        out_specs=pl.BlockSpec((1, N_PREP, nc, 8, L), lambda b: (b, 0, 0, 0, 0)),
        out_shape=jax.ShapeDtypeStruct((bsz, N_PREP, nc, 8, L), F32),
        scratch_shapes=[pltpu.VMEM((nc, 8, L), F32)],
        compiler_params=_params(("parallel",)),
        name="mlstm_gate_prep",
    )(gt, bias_rows)


def _scan_body(qf_ref, ktf_ref, vf_ref, pf_ref, qb_ref, ktb_ref, vb_ref, pb_ref, hf_ref, hb_ref, st_ref):
    L = CHUNK
    cb = pf_ref.shape[2]

    @pl.when(pl.program_id(1) == 0)
    def _():
        st_ref[...] = jnp.zeros(st_ref.shape, F32)

    t_idx = lax.broadcasted_iota(jnp.int32, (L, 2 * L), 0)
    s_idx = lax.broadcasted_iota(jnp.int32, (L, 2 * L), 1)
    masks = ((s_idx <= t_idx) | (s_idx >= L), (s_idx >= t_idx))
    ones_blk = jnp.ones((L, L), BF)

    dirs = ((qf_ref, ktf_ref, vf_ref, pf_ref, hf_ref), (qb_ref, ktb_ref, vb_ref, pb_ref, hb_ref))

    def body(ci, carry):
        probs = []
        for d, (q_ref, kt_ref, v_ref, p_ref, h_ref) in enumerate(dirs):
            c = ci if d == 0 else cb - 1 - ci
            rows = pl.ds(pl.multiple_of(c * L, L), L)
            a2, mprev2, mm2, eneg, w, decay = [p_ref[0, qi, c] for qi in range(6)]
            mm2_cols = mm2.T
            eneg_cols = eneg.T
            for hd in range(HEADS):
                rix = d * HEADS + hd
                qc = q_ref[0, rows, hd * DK:(hd + 1) * DK]
                ktc = kt_ref[0, hd * DK:(hd + 1) * DK, rows]
                vc = v_ref[0, rows, hd * DV:(hd + 1) * DV]
                probs.append(dict(
                    d=d, hd=hd, rows=rows, h_ref=h_ref, qc=qc, ktc=ktc, s=_dot(qc, ktc),
                    top=jnp.concatenate([vc, ones_blk], axis=1),
                    rowvec=jnp.concatenate([a2[rix:rix + 1], mprev2[rix:rix + 1]], axis=1),
                    mm2=mm2_cols[:, rix:rix + 1], eneg=eneg_cols[:, rix:rix + 1],
                    w=w[rix:rix + 1], dec=decay[rix:rix + 1]))
        for pr in probs:
            dmat = jnp.where(masks[pr["d"]], jnp.exp2(pr["rowvec"] - pr["mm2"]), 0.0)
            lhs = (jnp.concatenate([pr["s"], pr["qc"].astype(F32)], axis=1) * dmat).astype(BF)
            st = st_ref[pr["d"], pr["hd"]]
            pr["main"] = _dot(lhs, jnp.concatenate([pr["top"], st.astype(BF)], axis=0))
            ktw = (pr["ktc"].astype(F32) * pr["w"]).astype(BF)
            dec = pr["dec"]
            pr["st_new"] = jnp.concatenate([dec, dec, dec], axis=1) * st + _dot(ktw, pr["top"])
        for pr in probs:
            main = pr["main"]
            r = 1.0 / jnp.maximum(jnp.abs(main[:, DV:]), pr["eneg"])
            h = main[:, 0:DV] * jnp.concatenate([r, r], axis=1)
            pr["h_ref"][0, pr["rows"], pr["hd"] * DV:(pr["hd"] + 1) * DV] = h.astype(BF)
            st_ref[pr["d"], pr["hd"]] = pr["st_new"]
        return carry

    lax.fori_loop(0, cb, body, 0)


def _scan(q, kt, v, prep, sb):
    bsz, s, _ = q.shape
    nb = s // sb
    cb = sb // CHUNK
    fwd3 = lambda b, j: (b, j, 0)
    bwd3 = lambda b, j: (b, nb - 1 - j, 0)
    in_specs = []
    for blk3, blkt, blkp in ((fwd3, lambda b, j: (b, 0, j), lambda b, j: (b, 0, j, 0, 0)),
                             (bwd3, lambda b, j: (b, 0, nb - 1 - j), lambda b, j: (b, 0, nb - 1 - j, 0, 0))):
        in_specs += [pl.BlockSpec((1, sb, QK_WIDTH), blk3), pl.BlockSpec((1, QK_WIDTH, sb), blkt),
                     pl.BlockSpec((1, sb, V_WIDTH), blk3), pl.BlockSpec((1, N_PREP, cb, 8, CHUNK), blkp)]
    return pl.pallas_call(
        _scan_body,
        grid=(bsz, nb),
        in_specs=in_specs,
        out_specs=[pl.BlockSpec((1, sb, V_WIDTH), fwd3), pl.BlockSpec((1, sb, V_WIDTH), bwd3)],
        out_shape=[jax.ShapeDtypeStruct((bsz, s, V_WIDTH), BF)] * 2,
        scratch_shapes=[pltpu.VMEM((2, HEADS, DK, DV + CHUNK), F32)],
        compiler_params=_params(("parallel", "arbitrary"), V7X_SCOPED_VMEM_BYTES),
        name="mlstm_scan",
    )(q, kt, v, prep, q, kt, v, prep)


def _mlp(xn, w1_ref, w2_ref, ff_chunk):
    d_ff = w1_ref.shape[2]
    acc = None
    for c0 in range(0, d_ff, ff_chunk):
        a = _dot(xn, w1_ref[0, :, c0:c0 + ff_chunk])
        a = jnp.square(jnp.maximum(a, 0.0)).astype(BF)
        part = _dot(a, w2_ref[0, c0:c0 + ff_chunk, :])
        acc = part if acc is None else acc + part
    return acc


def _mlp_and_embed(h, p_ref, nmlp_ref, w1_ref, w2_ref, nple_ref, gw_ref, gb_ref, plew_ref, ff_chunk):
    h = h + _mlp(_rmsnorm(h, nmlp_ref[...]).astype(BF), w1_ref, w2_ref, ff_chunk)
    gate = jax.nn.sigmoid(_dot(_rmsnorm(h, nple_ref[...]).astype(BF), gw_ref[0]) + gb_ref[...])
    return h + gate * _dot(p_ref[...].astype(BF), plew_ref[0])


def _tail0_body(hf_ref, hb_ref, so_ref, x_ref, p_ref, hnorm_ref, wout_ref, nmlp_ref, w1_ref, w2_ref, nple_ref,
                gw_ref, gb_ref, plew_ref, nmix_ref, pin_ref, h_ref, u_ref, *, ff_chunk):
    hs = hf_ref[...].astype(F32) + hb_ref[...].astype(F32)
    parts = []
    for hd in range(HEADS):
        blk = hs[:, hd * DV:(hd + 1) * DV]
        ms = jnp.mean(blk * blk, axis=-1, keepdims=True)
        parts.append(blk * lax.rsqrt(ms + EPS))
    hn = jnp.concatenate(parts, axis=1) * hnorm_ref[...] * so_ref[...].astype(F32)
    h = x_ref[...] + _dot(hn.astype(BF), wout_ref[...])
    h = _mlp_and_embed(h, p_ref, nmlp_ref, w1_ref, w2_ref, nple_ref, gw_ref, gb_ref, plew_ref, ff_chunk)
    h_ref[...] = h
    u_ref[...] = _dot(_rmsnorm(h, nmix_ref[...]).astype(BF), pin_ref[...])


def _tail0(hf, hb, so, x2, p2, weights, tm, ff_chunk):
    m, d = x2.shape
    return pl.pallas_call(
        functools.partial(_tail0_body, ff_chunk=ff_chunk),
        grid=(m // tm,),
        in_specs=[_rows(tm, d), _rows(tm, d), _rows(tm, d), _rows(tm, d), _rows(tm, p2.shape[1])]
                 + [spec for _, spec in weights],
        out_specs=[_rows(tm, d), _rows(tm, d)],
        out_shape=[jax.ShapeDtypeStruct((m, d), F32), jax.ShapeDtypeStruct((m, d), F32)],
        compiler_params=_params(("parallel",), V7X_SCOPED_VMEM_BYTES),
        name="layer0_tail",
    )(hf, hb, so, x2, p2, *[w for w, _ in weights])


def _layer1_body(u_ref, uprev_ref, unext_ref, h_ref, p_ref, wgrp_ref, scale_ref, pout_ref, nmlp_ref, w1_ref,
                 w2_ref, nple_ref, gw_ref, gb_ref, plew_ref, nfin_ref, out_ref, ext_ref, lvl_ref, *, seq, ff_chunk):
    tm, d = u_ref.shape
    gw = d // len(POOL_WINDOWS)
    rows = tm + 2 * POOL_HALO
    t0 = (pl.program_id(0) * tm) % seq
    ext_ref[0:POOL_HALO, :] = jnp.where(t0 > 0, uprev_ref[...], 0.0)
    ext_ref[POOL_HALO:POOL_HALO + tm, :] = u_ref[...]
    ext_ref[POOL_HALO + tm:rows, :] = jnp.where(t0 + tm < seq, unext_ref[...], 0.0)
    ext_ref[rows:, :] = jnp.zeros((POOL_HALO, d), F32)
    lvl_ref[:, rows:, :] = jnp.zeros((2, POOL_HALO, gw), F32)

    t = t0 + lax.broadcasted_iota(jnp.int32, (tm, LANES), 0)
    mixed = []
    for gi, win in enumerate(POOL_WINDOWS):
        cols = slice(gi * gw, (gi + 1) * gw)
        load = lambda off, n, cols=cols: ext_ref[pl.ds(off, n), cols]
        span, slot = 1, 0
        while 2 * span < win:
            lvl_ref[slot, 0:rows, :] = load(0, rows) + load(span, rows)
            load = lambda off, n, slot=slot: lvl_ref[slot, pl.ds(off, n), :]
            span, slot = 2 * span, 1 - slot
        start = POOL_HALO - win // 2
        total = load(start, tm) + load(start + span, tm)
        cnt = (jnp.minimum(t + (win - win // 2), seq) - jnp.maximum(t - win // 2, 0)).astype(F32)
        cnt = jnp.concatenate([cnt] * (gw // LANES), axis=1)
        y = total / cnt - u_ref[:, cols]
        mixed.append(_dot(y.astype(BF), wgrp_ref[gi]))
    y = jnp.concatenate(mixed, axis=1) * scale_ref[...]
    h = h_ref[...] + _dot(y.astype(BF), pout_ref[...])
    h = _mlp_and_embed(h, p_ref, nmlp_ref, w1_ref, w2_ref, nple_ref, gw_ref, gb_ref, plew_ref, ff_chunk)
    out_ref[...] = _rmsnorm(h, nfin_ref[...])


def _layer1(u, h, p2, weights, tm, seq, ff_chunk):
    m, d = u.shape
    n_tiles = m // tm
    hpt = tm // POOL_HALO
    last_halo_block = m // POOL_HALO - 1
    gw = d // len(POOL_WINDOWS)
    return pl.pallas_call(
        functools.partial(_layer1_body, seq=seq, ff_chunk=ff_chunk),
        grid=(n_tiles,),
        in_specs=[_rows(tm, d),
                  pl.BlockSpec((POOL_HALO, d), lambda i: (jnp.maximum(i * hpt - 1, 0), 0)),
                  pl.BlockSpec((POOL_HALO, d), lambda i: (jnp.minimum((i + 1) * hpt, last_halo_block), 0)),
                  _rows(tm, d), _rows(tm, p2.shape[1], first_tile=n_tiles)]
                 + [spec for _, spec in weights],
        out_specs=_rows(tm, d),
        out_shape=jax.ShapeDtypeStruct((m, d), F32),
        scratch_shapes=[pltpu.VMEM((tm + 3 * POOL_HALO, d), F32), pltpu.VMEM((2, tm + 3 * POOL_HALO, gw), F32)],
        compiler_params=_params(("parallel",), V7X_SCOPED_VMEM_BYTES),
        name="layer1_pool_mlp",
    )(u, u, u, h, p2, *[w for w, _ in weights])


def kernel(x, p, norm_mix, norm_mlp, norm_ple, norm_final, mlstm_w_in, mlstm_b_gates, mlstm_head_norm,
           mlstm_w_out, pool_w_in, pool_w_grp, pool_scale, pool_w_out, mlp_w1, mlp_w2, ple_w, ple_gate_w,
           ple_gate_b):
    bsz, seq, d = x.shape
    m = bsz * seq
    nc = seq // CHUNK
    tm_proj, tm_fused, ff_chunk, scan_block = min(512, seq), min(512, seq), 1024, min(1024, seq)
    assert seq % scan_block == 0 and seq % tm_fused == 0 and seq % tm_proj == 0 and scan_block % CHUNK == 0
    row = lambda v: v.reshape(1, -1)
    bf = lambda w: w.astype(BF)

    def whole(v):
        return v, _resident(v.shape)

    def layer(stacked, i):
        return stacked, _layer_of(stacked, i)

    x2 = x.reshape(m, d)
    p2 = p.reshape(-1, p.shape[-1])
    n_main = 2 * QK_WIDTH + 2 * V_WIDTH
    w_in = bf(mlstm_w_in)
    w_kt = w_in[0, :, QK_WIDTH:2 * QK_WIDTH].T
    gate_perm = jnp.array([0, 1, 2, 3, 8, 9, 10, 11, 4, 5, 6, 7, 12, 13, 14, 15], jnp.int32)
    w_gt = w_in[0, :, n_main:].T[gate_perm]
    bias_rows = jnp.broadcast_to(mlstm_b_gates[0].reshape(N_GATES)[gate_perm][:, None], (N_GATES, CHUNK))

    q, kt, v, so, gt = _inproj(x2, row(norm_mix[0]), w_in, w_kt, w_gt, tm_proj, seq)
    prep = _prep(gt, bias_rows)
    hf, hb = _scan(q.reshape(bsz, seq, QK_WIDTH), kt, v.reshape(bsz, seq, V_WIDTH), prep, scan_block)

    w1, w2, gate_w, emb_w = bf(mlp_w1), bf(mlp_w2), bf(ple_gate_w), bf(ple_w)

    def mlp_embed_weights(i):
        return [whole(row(norm_mlp[i])), layer(w1, i), layer(w2, i), whole(row(norm_ple[i])), layer(gate_w, i),
                whole(row(ple_gate_b[i])), layer(emb_w, i)]

    tail_weights = ([whole(row(mlstm_head_norm[0])), whole(bf(mlstm_w_out[0]))] + mlp_embed_weights(0)
                    + [whole(row(norm_mix[1])), whole(bf(pool_w_in[0]))])
    h, u = _tail0(hf.reshape(m, V_WIDTH), hb.reshape(m, V_WIDTH), so, x2, p2, tail_weights, tm_fused, ff_chunk)

    l1_weights = ([whole(bf(pool_w_grp[0])), whole(row(pool_scale[0])), whole(bf(pool_w_out[0]))]
                  + mlp_embed_weights(1) + [whole(row(norm_final))])
    out = _layer1(u, h, p2, l1_weights, tm_fused, seq, ff_chunk)
    return out.reshape(bsz, seq, d)
```

```python
import functools

import jax
import jax.numpy as jnp
from jax import lax
from jax.experimental import pallas as pl
from jax.experimental.pallas import tpu as pltpu

EPS = 1e-6
HEADS = 4
DK = 128
DV = 256
CHUNK = 128
QK_WIDTH = HEADS * DK
V_WIDTH = HEADS * DV
N_GATES = 4 * HEADS
POOL_WINDOWS = (2, 4, 8, 16)
POOL_HALO = 8
N_PREP = 8

LANES = 128
SUBLANES = 8
V7X_SCOPED_VMEM_BYTES = 60000 * 1024

LOG2E = 1.4426950408889634

BF = jnp.bfloat16
F32 = jnp.float32


def _dot(a, b):
    return jnp.dot(a, b, preferred_element_type=F32)


def _rmsnorm(x, gain):
    ms = jnp.mean(x * x, axis=-1, keepdims=True)
    return x * lax.rsqrt(ms + EPS) * gain


def _resident(shape):
    nd = len(shape)
    return pl.BlockSpec(shape, lambda *_: (0,) * nd, pipeline_mode=pl.Buffered(1))


def _layer_of(stacked, layer):
    nd = stacked.ndim
    return pl.BlockSpec((1,) + stacked.shape[1:], lambda *_: (layer,) + (0,) * (nd - 1),
                        pipeline_mode=pl.Buffered(1))


def _rows(tm, width, first_tile=0):
    return pl.BlockSpec((tm, width), lambda i: (i + first_tile, 0))


def _params(sem, vmem=None):
    return pltpu.CompilerParams(dimension_semantics=sem, vmem_limit_bytes=vmem)


def _inproj_body(x_ref, gain_ref, w_ref, wkt_ref, wgt_ref, q_ref, kt_ref, v_ref, so_ref, gt_ref):
    v0 = 2 * QK_WIDTH
    nt = (((1,), (1,)), ((), ()))
    xn = _rmsnorm(x_ref[...], gain_ref[...]).astype(BF)
    q_ref[...] = _dot(xn, w_ref[0, :, 0:QK_WIDTH]).astype(BF)
    kt = lax.dot_general(wkt_ref[...], xn, nt, preferred_element_type=F32)
    kt_ref[0] = (kt * (DK ** -0.5)).astype(BF)
    v_ref[...] = _dot(xn, w_ref[0, :, v0:v0 + V_WIDTH]).astype(BF)
    so_ref[...] = jax.nn.sigmoid(_dot(xn, w_ref[0, :, v0 + V_WIDTH:v0 + 2 * V_WIDTH])).astype(BF)
    gt = lax.dot_general(wgt_ref[...], xn, nt, preferred_element_type=F32)
    for c in range(gt_ref.shape[1]):
        gt_ref[0, c] = gt[:, c * CHUNK:(c + 1) * CHUNK]


def _inproj(x2, gain, w_in, w_kt, w_gt, tm, seq):
    m, d = x2.shape
    tiles_per_seq = seq // tm
    cpt = tm // CHUNK
    per_seq = lambda i: (i // tiles_per_seq, i % tiles_per_seq)
    return pl.pallas_call(
        _inproj_body,
        grid=(m // tm,),
        in_specs=[_rows(tm, d), _resident(gain.shape), _layer_of(w_in, 0), _resident(w_kt.shape),
                  _resident(w_gt.shape)],
        out_specs=[_rows(tm, QK_WIDTH),
                   pl.BlockSpec((1, QK_WIDTH, tm), lambda i: (per_seq(i)[0], 0, per_seq(i)[1])),
                   _rows(tm, V_WIDTH), _rows(tm, V_WIDTH),
                   pl.BlockSpec((1, cpt, N_GATES, CHUNK), lambda i: (per_seq(i)[0], per_seq(i)[1], 0, 0))],
        out_shape=[jax.ShapeDtypeStruct((m, QK_WIDTH), BF), jax.ShapeDtypeStruct((m // seq, QK_WIDTH, seq), BF),
                   jax.ShapeDtypeStruct((m, V_WIDTH), BF), jax.ShapeDtypeStruct((m, V_WIDTH), BF),
                   jax.ShapeDtypeStruct((m // seq, seq // CHUNK, N_GATES, CHUNK), F32)],
        compiler_params=_params(("parallel",), V7X_SCOPED_VMEM_BYTES),
        name="mlstm_inproj",
    )(x2, gain, w_in, w_kt, w_gt)


def _log_sigmoid(x):
    return jnp.minimum(x, 0.0) - jnp.log1p(jnp.exp(-jnp.abs(x)))


def _split3(x):
    hi = x.astype(BF)
    r1 = x - hi.astype(F32)
    mid = r1.astype(BF)
    lo = (r1 - mid.astype(F32)).astype(BF)
    return hi, mid, lo


def _prep_body(gt_ref, bias_ref, out_ref, mprev_ref):
    nc = gt_ref.shape[1]
    L = CHUNK
    g = gt_ref[0] + bias_ref[...][None]
    i_pre = g[:, 0:8, :].reshape(nc * 8, L)
    logf = _log_sigmoid(g[:, 8:16, :]).reshape(nc * 8, L)

    row = lax.broadcasted_iota(jnp.int32, (nc * 8, L), 0)
    lane = lax.broadcasted_iota(jnp.int32, (nc * 8, L), 1)
    is_fwd = (row % 8) < HEADS

    s_idx = lax.broadcasted_iota(jnp.int32, (L, L), 0)
    t_idx = lax.broadcasted_iota(jnp.int32, (L, L), 1)
    tri_pre = (s_idx <= t_idx).astype(BF)
    tri_suf = (s_idx >= t_idx).astype(BF)
    pre = jnp.zeros((nc * 8, L), F32)
    suf = jnp.zeros((nc * 8, L), F32)
    for piece in _split3(logf):
        pre = pre + _dot(piece, tri_pre)
        suf = suf + _dot(piece, tri_suf)
    b = jnp.where(is_fwd, pre, suf)
    a = i_pre - b

    pm = a
    sm = a
    k = 1
    while k < L:
        pm = jnp.where(lane >= k, jnp.maximum(pm, pltpu.roll(pm, k, 1)), pm)
        sm = jnp.where(lane < L - k, jnp.maximum(sm, pltpu.roll(sm, L - k, 1)), sm)
        k *= 2
    cm = jnp.where(is_fwd, pm, sm)

    amax = jnp.broadcast_to(jnp.max(a, axis=1, keepdims=True), (nc * 8, L)).reshape(nc, 8, L)
    b_last = jnp.broadcast_to(jnp.sum(logf, axis=1, keepdims=True), (nc * 8, L)).reshape(nc, 8, L)

    row8 = lax.broadcasted_iota(jnp.int32, (8, L), 0) < HEADS
    m = jnp.zeros((8, L), F32)
    for c in range(nc):
        cb = nc - 1 - c
        mprev_ref[c, 0:HEADS, :] = m[0:HEADS]
        mprev_ref[cb, HEADS:8, :] = m[HEADS:8]
        am = jnp.where(row8, amax[c], amax[cb])
        bl = jnp.where(row8, b_last[c], b_last[cb])
        m = bl + jnp.maximum(m, am)

    mprev = mprev_ref[...]
    a3 = a.reshape(nc, 8, L)
    b3 = b.reshape(nc, 8, L)
    cm3 = cm.reshape(nc, 8, L)
    sigma = jnp.maximum(mprev, amax)
    mm = jnp.maximum(mprev, cm3)
    out_ref[0, 0] = a3 * LOG2E
    out_ref[0, 1] = mprev * LOG2E
    out_ref[0, 2] = mm * LOG2E
    out_ref[0, 3] = jnp.exp(-(b3 + mm))
    out_ref[0, 4] = jnp.exp(a3 - sigma)
    out_ref[0, 5] = jnp.exp(mprev - sigma)
    out_ref[0, 6] = jnp.zeros((nc, 8, L), F32)
    out_ref[0, 7] = jnp.zeros((nc, 8, L), F32)


def _prep(gt, bias_rows):
    bsz, nc, _, L = gt.shape
    return pl.pallas_call(
        _prep_body,
        grid=(bsz,),
        in_specs=[pl.BlockSpec((1, nc, N_GATES, L), lambda b: (b, 0, 0, 0)), _resident(bias_rows.shape)],
        out_specs=pl.BlockSpec((1, N_PREP, nc, 8, L), lambda b: (b, 0, 0, 0, 0)),
        out_shape=jax.ShapeDtypeStruct((bsz, N_PREP, nc, 8, L), F32),
        scratch_shapes=[pltpu.VMEM((nc, 8, L), F32)],
        compiler_params=_params(("parallel",)),
        name="mlstm_gate_prep",
    )(gt, bias_rows)


def _scan_body(qf_ref, ktf_ref, vf_ref, pf_ref, qb_ref, ktb_ref, vb_ref, pb_ref, hf_ref, hb_ref, st_ref):
    L = CHUNK
    cb = pf_ref.shape[2]

    @pl.when(pl.program_id(1) == 0)
    def _():
        st_ref[...] = jnp.zeros(st_ref.shape, F32)

    t_idx = lax.broadcasted_iota(jnp.int32, (L, 2 * L), 0)
    s_idx = lax.broadcasted_iota(jnp.int32, (L, 2 * L), 1)
    masks = ((s_idx <= t_idx) | (s_idx >= L), (s_idx >= t_idx))
    ones_blk = jnp.ones((L, L), BF)

    dirs = ((qf_ref, ktf_ref, vf_ref, pf_ref, hf_ref), (qb_ref, ktb_ref, vb_ref, pb_ref, hb_ref))

    def body(ci, carry):
        probs = []
        for d, (q_ref, kt_ref, v_ref, p_ref, h_ref) in enumerate(dirs):
            c = ci if d == 0 else cb - 1 - ci
            rows = pl.ds(pl.multiple_of(c * L, L), L)
            a2, mprev2, mm2, eneg, w, decay = [p_ref[0, qi, c] for qi in range(6)]
            mm2_cols = mm2.T
            eneg_cols = eneg.T
            for hd in range(HEADS):
                rix = d * HEADS + hd
                qc = q_ref[0, rows, hd * DK:(hd + 1) * DK]
                ktc = kt_ref[0, hd * DK:(hd + 1) * DK, rows]
                vc = v_ref[0, rows, hd * DV:(hd + 1) * DV]
                probs.append(dict(
                    d=d, hd=hd, rows=rows, h_ref=h_ref, qc=qc, ktc=ktc, s=_dot(qc, ktc),
                    top=jnp.concatenate([vc, ones_blk], axis=1),
                    rowvec=jnp.concatenate([a2[rix:rix + 1], mprev2[rix:rix + 1]], axis=1),
                    mm2=mm2_cols[:, rix:rix + 1], eneg=eneg_cols[:, rix:rix + 1],
                    w=w[rix:rix + 1], dec=decay[rix:rix + 1]))
        for pr in probs:
            dmat = jnp.where(masks[pr["d"]], jnp.exp2(pr["rowvec"] - pr["mm2"]), 0.0)
            lhs = (jnp.concatenate([pr["s"], pr["qc"].astype(F32)], axis=1) * dmat).astype(BF)
            st = st_ref[pr["d"], pr["hd"]]
            pr["main"] = _dot(lhs, jnp.concatenate([pr["top"], st.astype(BF)], axis=0))
            ktw = (pr["ktc"].astype(F32) * pr["w"]).astype(BF)
            dec = pr["dec"]
            pr["st_new"] = jnp.concatenate([dec, dec, dec], axis=1) * st + _dot(ktw, pr["top"])
        for pr in probs:
            main = pr["main"]
            r = 1.0 / jnp.maximum(jnp.abs(main[:, DV:]), pr["eneg"])
            h = main[:, 0:DV] * jnp.concatenate([r, r], axis=1)
            pr["h_ref"][0, pr["rows"], pr["hd"] * DV:(pr["hd"] + 1) * DV] = h.astype(BF)
            st_ref[pr["d"], pr["hd"]] = pr["st_new"]
        return carry

    lax.fori_loop(0, cb, body, 0)


def _scan(q, kt, v, prep, sb):
    bsz, s, _ = q.shape
    nb = s // sb
    cb = sb // CHUNK
    fwd3 = lambda b, j: (b, j, 0)
    bwd3 = lambda b, j: (b, nb - 1 - j, 0)
    in_specs = []
    for blk3, blkt, blkp in ((fwd3, lambda b, j: (b, 0, j), lambda b, j: (b, 0, j, 0, 0)),
                             (bwd3, lambda b, j: (b, 0, nb - 1 - j), lambda b, j: (b, 0, nb - 1 - j, 0, 0))):
        in_specs += [pl.BlockSpec((1, sb, QK_WIDTH), blk3), pl.BlockSpec((1, QK_WIDTH, sb), blkt),
                     pl.BlockSpec((1, sb, V_WIDTH), blk3), pl.BlockSpec((1, N_PREP, cb, 8, CHUNK), blkp)]
    return pl.pallas_call(
        _scan_body,
        grid=(bsz, nb),
        in_specs=in_specs,
        out_specs=[pl.BlockSpec((1, sb, V_WIDTH), fwd3), pl.BlockSpec((1, sb, V_WIDTH), bwd3)],
        out_shape=[jax.ShapeDtypeStruct((bsz, s, V_WIDTH), BF)] * 2,
        scratch_shapes=[pltpu.VMEM((2, HEADS, DK, DV + CHUNK), F32)],
        compiler_params=_params(("parallel", "arbitrary"), V7X_SCOPED_VMEM_BYTES),
        name="mlstm_scan",
    )(q, kt, v, prep, q, kt, v, prep)


def _mlp(xn, w1_ref, w2_ref, ff_chunk):
    d_ff = w1_ref.shape[2]
    acc = None
    for c0 in range(0, d_ff, ff_chunk):
        a = _dot(xn, w1_ref[0, :, c0:c0 + ff_chunk])
        a = jnp.square(jnp.maximum(a, 0.0)).astype(BF)
        part = _dot(a, w2_ref[0, c0:c0 + ff_chunk, :])
        acc = part if acc is None else acc + part
    return acc


def _mlp_and_embed(h, p_ref, nmlp_ref, w1_ref, w2_ref, nple_ref, gw_ref, gb_ref, plew_ref, ff_chunk):
    h = h + _mlp(_rmsnorm(h, nmlp_ref[...]).astype(BF), w1_ref, w2_ref, ff_chunk)
    gate = jax.nn.sigmoid(_dot(_rmsnorm(h, nple_ref[...]).astype(BF), gw_ref[0]) + gb_ref[...])
    return h + gate * _dot(p_ref[...].astype(BF), plew_ref[0])


def _tail0_body(hf_ref, hb_ref, so_ref, x_ref, p_ref, hnorm_ref, wout_ref, nmlp_ref, w1_ref, w2_ref, nple_ref,
                gw_ref, gb_ref, plew_ref, nmix_ref, pin_ref, h_ref, u_ref, *, ff_chunk):
    hs = hf_ref[...].astype(F32) + hb_ref[...].astype(F32)
    parts = []
    for hd in range(HEADS):
        blk = hs[:, hd * DV:(hd + 1) * DV]
        ms = jnp.mean(blk * blk, axis=-1, keepdims=True)
        parts.append(blk * lax.rsqrt(ms + EPS))
    hn = jnp.concatenate(parts, axis=1) * hnorm_ref[...] * so_ref[...].astype(F32)
    h = x_ref[...] + _dot(hn.astype(BF), wout_ref[...])
    h = _mlp_and_embed(h, p_ref, nmlp_ref, w1_ref, w2_ref, nple_ref, gw_ref, gb_ref, plew_ref, ff_chunk)
    h_ref[...] = h
    u_ref[...] = _dot(_rmsnorm(h, nmix_ref[...]).astype(BF), pin_ref[...])


def _tail0(hf, hb, so, x2, p2, weights, tm, ff_chunk):
    m, d = x2.shape
    return pl.pallas_call(
        functools.partial(_tail0_body, ff_chunk=ff_chunk),
        grid=(m // tm,),
        in_specs=[_rows(tm, d), _rows(tm, d), _rows(tm, d), _rows(tm, d), _rows(tm, p2.shape[1])]
                 + [spec for _, spec in weights],
        out_specs=[_rows(tm, d), _rows(tm, d)],
        out_shape=[jax.ShapeDtypeStruct((m, d), F32), jax.ShapeDtypeStruct((m, d), F32)],
        compiler_params=_params(("parallel",), V7X_SCOPED_VMEM_BYTES),
        name="layer0_tail",
    )(hf, hb, so, x2, p2, *[w for w, _ in weights])


def _layer1_body(u_ref, uprev_ref, unext_ref, h_ref, p_ref, wgrp_ref, scale_ref, pout_ref, nmlp_ref, w1_ref,
                 w2_ref, nple_ref, gw_ref, gb_ref, plew_ref, nfin_ref, out_ref, ext_ref, lvl_ref, *, seq, ff_chunk):
    tm, d = u_ref.shape
    gw = d // len(POOL_WINDOWS)
    rows = tm + 2 * POOL_HALO
    t0 = (pl.program_id(0) * tm) % seq
    ext_ref[0:POOL_HALO, :] = jnp.where(t0 > 0, uprev_ref[...], 0.0)
    ext_ref[POOL_HALO:POOL_HALO + tm, :] = u_ref[...]
    ext_ref[POOL_HALO + tm:rows, :] = jnp.where(t0 + tm < seq, unext_ref[...], 0.0)
    ext_ref[rows:, :] = jnp.zeros((POOL_HALO, d), F32)
    lvl_ref[:, rows:, :] = jnp.zeros((2, POOL_HALO, gw), F32)

    t = t0 + lax.broadcasted_iota(jnp.int32, (tm, LANES), 0)
    mixed = []
    for gi, win in enumerate(POOL_WINDOWS):
        cols = slice(gi * gw, (gi + 1) * gw)
        load = lambda off, n, cols=cols: ext_ref[pl.ds(off, n), cols]
        span, slot = 1, 0
        while 2 * span < win:
            lvl_ref[slot, 0:rows, :] = load(0, rows) + load(span, rows)
            load = lambda off, n, slot=slot: lvl_ref[slot, pl.ds(off, n), :]
            span, slot = 2 * span, 1 - slot
        start = POOL_HALO - win // 2
        total = load(start, tm) + load(start + span, tm)
        cnt = (jnp.minimum(t + (win - win // 2), seq) - jnp.maximum(t - win // 2, 0)).astype(F32)
        cnt = jnp.concatenate([cnt] * (gw // LANES), axis=1)
        y = total / cnt - u_ref[:, cols]
        mixed.append(_dot(y.astype(BF), wgrp_ref[gi]))
    y = jnp.concatenate(mixed, axis=1) * scale_ref[...]
    h = h_ref[...] + _dot(y.astype(BF), pout_ref[...])
    h = _mlp_and_embed(h, p_ref, nmlp_ref, w1_ref, w2_ref, nple_ref, gw_ref, gb_ref, plew_ref, ff_chunk)
    out_ref[...] = _rmsnorm(h, nfin_ref[...])


def _layer1(u, h, p2, weights, tm, seq, ff_chunk):
    m, d = u.shape
    n_tiles = m // tm
    hpt = tm // POOL_HALO
    last_halo_block = m // POOL_HALO - 1
    gw = d // len(POOL_WINDOWS)
    return pl.pallas_call(
        functools.partial(_layer1_body, seq=seq, ff_chunk=ff_chunk),
        grid=(n_tiles,),
        in_specs=[_rows(tm, d),
                  pl.BlockSpec((POOL_HALO, d), lambda i: (jnp.maximum(i * hpt - 1, 0), 0)),
                  pl.BlockSpec((POOL_HALO, d), lambda i: (jnp.minimum((i + 1) * hpt, last_halo_block), 0)),
                  _rows(tm, d), _rows(tm, p2.shape[1], first_tile=n_tiles)]
                 + [spec for _, spec in weights],
        out_specs=_rows(tm, d),
        out_shape=jax.ShapeDtypeStruct((m, d), F32),
        scratch_shapes=[pltpu.VMEM((tm + 3 * POOL_HALO, d), F32), pltpu.VMEM((2, tm + 3 * POOL_HALO, gw), F32)],
        compiler_params=_params(("parallel",), V7X_SCOPED_VMEM_BYTES),
        name="layer1_pool_mlp",
    )(u, u, u, h, p2, *[w for w, _ in weights])


def kernel(x, p, norm_mix, norm_mlp, norm_ple, norm_final, mlstm_w_in, mlstm_b_gates, mlstm_head_norm,
           mlstm_w_out, pool_w_in, pool_w_grp, pool_scale, pool_w_out, mlp_w1, mlp_w2, ple_w, ple_gate_w,
           ple_gate_b):
    bsz, seq, d = x.shape
    m = bsz * seq
    nc = seq // CHUNK
    tm_proj, tm_fused, ff_chunk, scan_block = min(1024, seq), min(512, seq), 1024, min(2048, seq)
    assert seq % scan_block == 0 and seq % tm_fused == 0 and seq % tm_proj == 0 and scan_block % CHUNK == 0
    row = lambda v: v.reshape(1, -1)
    bf = lambda w: w.astype(BF)

    def whole(v):
        return v, _resident(v.shape)

    def layer(stacked, i):
        return stacked, _layer_of(stacked, i)

    x2 = x.reshape(m, d)
    p2 = p.reshape(-1, p.shape[-1])
    n_main = 2 * QK_WIDTH + 2 * V_WIDTH
    w_in = bf(mlstm_w_in)
    w_kt = w_in[0, :, QK_WIDTH:2 * QK_WIDTH].T
    gate_perm = jnp.array([0, 1, 2, 3, 8, 9, 10, 11, 4, 5, 6, 7, 12, 13, 14, 15], jnp.int32)
    w_gt = w_in[0, :, n_main:].T[gate_perm]
    bias_rows = jnp.broadcast_to(mlstm_b_gates[0].reshape(N_GATES)[gate_perm][:, None], (N_GATES, CHUNK))

    q, kt, v, so, gt = _inproj(x2, row(norm_mix[0]), w_in, w_kt, w_gt, tm_proj, seq)
    prep = _prep(gt, bias_rows)
    hf, hb = _scan(q.reshape(bsz, seq, QK_WIDTH), kt, v.reshape(bsz, seq, V_WIDTH), prep, scan_block)

    w1, w2, gate_w, emb_w = bf(mlp_w1), bf(mlp_w2), bf(ple_gate_w), bf(ple_w)

    def mlp_embed_weights(i):
        return [whole(row(norm_mlp[i])), layer(w1, i), layer(w2, i), whole(row(norm_ple[i])), layer(gate_w, i),
                whole(row(ple_gate_b[i])), layer(emb_w, i)]

    tail_weights = ([whole(row(mlstm_head_norm[0])), whole(bf(mlstm_w_out[0]))] + mlp_embed_weights(0)
                    + [whole(row(norm_mix[1])), whole(bf(pool_w_in[0]))])
    h, u = _tail0(hf.reshape(m, V_WIDTH), hb.reshape(m, V_WIDTH), so, x2, p2, tail_weights, tm_fused, ff_chunk)

    l1_weights = ([whole(bf(pool_w_grp[0])), whole(row(pool_scale[0])), whole(bf(pool_w_out[0]))]
                  + mlp_embed_weights(1) + [whole(row(norm_final))])
    out = _layer1(u, h, p2, l1_weights, tm_fused, seq, ff_chunk)
    return out.reshape(bsz, seq, d)
```

```python
import functools

import jax
import jax.numpy as jnp
from jax import lax
from jax.experimental import pallas as pl
from jax.experimental.pallas import tpu as pltpu

EPS = 1e-6
HEADS = 4
DK = 128
DV = 256
CHUNK = 128
QK_WIDTH = HEADS * DK
V_WIDTH = HEADS * DV
N_GATES = 4 * HEADS
POOL_WINDOWS = (2, 4, 8, 16)
POOL_HALO = 8
N_PREP = 8
CHUNKS_PER_TRIP = 2

LANES = 128
SUBLANES = 8
V7X_SCOPED_VMEM_BYTES = 60000 * 1024

LOG2E = 1.4426950408889634

BF = jnp.bfloat16
F32 = jnp.float32


def _dot(a, b):
    return jnp.dot(a, b, preferred_element_type=F32)


def _rmsnorm(x, gain):
    ms = jnp.mean(x * x, axis=-1, keepdims=True)
    return x * lax.rsqrt(ms + EPS) * gain


def _resident(shape):
    nd = len(shape)
    return pl.BlockSpec(shape, lambda *_: (0,) * nd, pipeline_mode=pl.Buffered(1))


def _layer_of(stacked, layer):
    nd = stacked.ndim
    return pl.BlockSpec((1,) + stacked.shape[1:], lambda *_: (layer,) + (0,) * (nd - 1),
                        pipeline_mode=pl.Buffered(1))


def _rows(tm, width, first_tile=0):
    return pl.BlockSpec((tm, width), lambda i: (i + first_tile, 0))


def _params(sem, vmem=None):
    return pltpu.CompilerParams(dimension_semantics=sem, vmem_limit_bytes=vmem)


def _inproj_body(x_ref, gain_ref, w_ref, wkt_ref, wgt_ref, q_ref, kt_ref, v_ref, so_ref, gt_ref):
    v0 = 2 * QK_WIDTH
    nt = (((1,), (1,)), ((), ()))
    xn = _rmsnorm(x_ref[...], gain_ref[...]).astype(BF)
    q_ref[...] = _dot(xn, w_ref[0, :, 0:QK_WIDTH]).astype(BF)
    kt = lax.dot_general(wkt_ref[...], xn, nt, preferred_element_type=F32)
    kt_ref[0] = (kt * (DK ** -0.5)).astype(BF)
    v_ref[...] = _dot(xn, w_ref[0, :, v0:v0 + V_WIDTH]).astype(BF)
    so_ref[...] = jax.nn.sigmoid(_dot(xn, w_ref[0, :, v0 + V_WIDTH:v0 + 2 * V_WIDTH])).astype(BF)
    gt = lax.dot_general(wgt_ref[...], xn, nt, preferred_element_type=F32)
    for c in range(gt_ref.shape[1]):
        gt_ref[0, c] = gt[:, c * CHUNK:(c + 1) * CHUNK]


def _inproj(x2, gain, w_in, w_kt, w_gt, tm, seq):
    m, d = x2.shape
    tiles_per_seq = seq // tm
    cpt = tm // CHUNK
    per_seq = lambda i: (i // tiles_per_seq, i % tiles_per_seq)
    return pl.pallas_call(
        _inproj_body,
        grid=(m // tm,),
        in_specs=[_rows(tm, d), _resident(gain.shape), _layer_of(w_in, 0), _resident(w_kt.shape),
                  _resident(w_gt.shape)],
        out_specs=[_rows(tm, QK_WIDTH),
                   pl.BlockSpec((1, QK_WIDTH, tm), lambda i: (per_seq(i)[0], 0, per_seq(i)[1])),
                   _rows(tm, V_WIDTH), _rows(tm, V_WIDTH),
                   pl.BlockSpec((1, cpt, N_GATES, CHUNK), lambda i: (per_seq(i)[0], per_seq(i)[1], 0, 0))],
        out_shape=[jax.ShapeDtypeStruct((m, QK_WIDTH), BF), jax.ShapeDtypeStruct((m // seq, QK_WIDTH, seq), BF),
                   jax.ShapeDtypeStruct((m, V_WIDTH), BF), jax.ShapeDtypeStruct((m, V_WIDTH), BF),
                   jax.ShapeDtypeStruct((m // seq, seq // CHUNK, N_GATES, CHUNK), F32)],
        compiler_params=_params(("parallel",), V7X_SCOPED_VMEM_BYTES),
        name="mlstm_inproj",
    )(x2, gain, w_in, w_kt, w_gt)


def _log_sigmoid(x):
    return jnp.minimum(x, 0.0) - jnp.log1p(jnp.exp(-jnp.abs(x)))


def _split3(x):
    hi = x.astype(BF)
    r1 = x - hi.astype(F32)
    mid = r1.astype(BF)
    lo = (r1 - mid.astype(F32)).astype(BF)
    return hi, mid, lo


def _prep_body(gt_ref, bias_ref, out_ref, mprev_ref):
    nc = gt_ref.shape[1]
    L = CHUNK
    g = gt_ref[0] + bias_ref[...][None]
    i_pre = g[:, 0:8, :].reshape(nc * 8, L)
    logf = _log_sigmoid(g[:, 8:16, :]).reshape(nc * 8, L)

    row = lax.broadcasted_iota(jnp.int32, (nc * 8, L), 0)
    lane = lax.broadcasted_iota(jnp.int32, (nc * 8, L), 1)
    is_fwd = (row % 8) < HEADS

    s_idx = lax.broadcasted_iota(jnp.int32, (L, L), 0)
    t_idx = lax.broadcasted_iota(jnp.int32, (L, L), 1)
    tri_pre = (s_idx <= t_idx).astype(BF)
    tri_suf = (s_idx >= t_idx).astype(BF)
    pre = jnp.zeros((nc * 8, L), F32)
    suf = jnp.zeros((nc * 8, L), F32)
    for piece in _split3(logf):
        pre = pre + _dot(piece, tri_pre)
        suf = suf + _dot(piece, tri_suf)
    b = jnp.where(is_fwd, pre, suf)
    a = i_pre - b

    pm = a
    sm = a
    k = 1
    while k < L:
        pm = jnp.where(lane >= k, jnp.maximum(pm, pltpu.roll(pm, k, 1)), pm)
        sm = jnp.where(lane < L - k, jnp.maximum(sm, pltpu.roll(sm, L - k, 1)), sm)
        k *= 2
    cm = jnp.where(is_fwd, pm, sm)

    amax = jnp.broadcast_to(jnp.max(a, axis=1, keepdims=True), (nc * 8, L)).reshape(nc, 8, L)
    b_last = jnp.broadcast_to(jnp.sum(logf, axis=1, keepdims=True), (nc * 8, L)).reshape(nc, 8, L)

    row8 = lax.broadcasted_iota(jnp.int32, (8, L), 0) < HEADS
    m = jnp.zeros((8, L), F32)
    for c in range(nc):
        cb = nc - 1 - c
        mprev_ref[c, 0:HEADS, :] = m[0:HEADS]
        mprev_ref[cb, HEADS:8, :] = m[HEADS:8]
        am = jnp.where(row8, amax[c], amax[cb])
        bl = jnp.where(row8, b_last[c], b_last[cb])
        m = bl + jnp.maximum(m, am)

    mprev = mprev_ref[...]
    a3 = a.reshape(nc, 8, L)
    b3 = b.reshape(nc, 8, L)
    cm3 = cm.reshape(nc, 8, L)
    sigma = jnp.maximum(mprev, amax)
    mm = jnp.maximum(mprev, cm3)
    out_ref[0, 0] = a3 * LOG2E
    out_ref[0, 1] = mprev * LOG2E
    out_ref[0, 2] = mm * LOG2E
    out_ref[0, 3] = jnp.exp(-(b3 + mm))
    out_ref[0, 4] = jnp.exp(a3 - sigma)
    out_ref[0, 5] = jnp.exp(mprev - sigma)
    out_ref[0, 6] = jnp.zeros((nc, 8, L), F32)
    out_ref[0, 7] = jnp.zeros((nc, 8, L), F32)


def _prep(gt, bias_rows):
    bsz, nc, _, L = gt.shape
    return pl.pallas_call(
        _prep_body,
        grid=(bsz,),
        in_specs=[pl.BlockSpec((1, nc, N_GATES, L), lambda b: (b, 0, 0, 0)), _resident(bias_rows.shape)],
        out_specs=pl.BlockSpec((1, N_PREP, nc, 8, L), lambda b: (b, 0, 0, 0, 0)),
        out_shape=jax.ShapeDtypeStruct((bsz, N_PREP, nc, 8, L), F32),
        scratch_shapes=[pltpu.VMEM((nc, 8, L), F32)],
        compiler_params=_params(("parallel",)),
        name="mlstm_gate_prep",
    )(gt, bias_rows)


def _scan_body(qf_ref, ktf_ref, vf_ref, pf_ref, qb_ref, ktb_ref, vb_ref, pb_ref, hf_ref, hb_ref, st_ref):
    L = CHUNK
    cb = pf_ref.shape[2]

    @pl.when(pl.program_id(1) == 0)
    def _():
        st_ref[...] = jnp.zeros(st_ref.shape, F32)

    t_idx = lax.broadcasted_iota(jnp.int32, (L, 2 * L), 0)
    s_idx = lax.broadcasted_iota(jnp.int32, (L, 2 * L), 1)
    masks = ((s_idx <= t_idx) | (s_idx >= L), (s_idx >= t_idx))
    ones_blk = jnp.ones((L, L), BF)

    dirs = ((qf_ref, ktf_ref, vf_ref, pf_ref, hf_ref), (qb_ref, ktb_ref, vb_ref, pb_ref, hb_ref))

    def body(ci, carry):
        probs = []
        for sub in range(CHUNKS_PER_TRIP):
            for d, (q_ref, kt_ref, v_ref, p_ref, h_ref) in enumerate(dirs):
                step = ci * CHUNKS_PER_TRIP + sub
                c = step if d == 0 else cb - 1 - step
                rows = pl.ds(pl.multiple_of(c * L, L), L)
                a2, mprev2, mm2, eneg, w, decay = [p_ref[0, qi, c] for qi in range(6)]
                mm2_cols = mm2.T
                eneg_cols = eneg.T
                for hd in range(HEADS):
                    rix = d * HEADS + hd
                    qc = q_ref[0, rows, hd * DK:(hd + 1) * DK]
                    ktc = kt_ref[0, hd * DK:(hd + 1) * DK, rows]
                    vc = v_ref[0, rows, hd * DV:(hd + 1) * DV]
                    probs.append(dict(
                        d=d, hd=hd, rows=rows, h_ref=h_ref, qc=qc, ktc=ktc, s=_dot(qc, ktc),
                        top=jnp.concatenate([vc, ones_blk], axis=1),
                        rowvec=jnp.concatenate([a2[rix:rix + 1], mprev2[rix:rix + 1]], axis=1),
                        mm2=mm2_cols[:, rix:rix + 1], eneg=eneg_cols[:, rix:rix + 1],
                        w=w[rix:rix + 1], dec=decay[rix:rix + 1]))
        state = {(d, hd): st_ref[d, hd] for d in range(2) for hd in range(HEADS)}
        for pr in probs:
            st = state[pr["d"], pr["hd"]]
            dmat = jnp.where(masks[pr["d"]], jnp.exp2(pr["rowvec"] - pr["mm2"]), 0.0)
            lhs = (jnp.concatenate([pr["s"], pr["qc"].astype(F32)], axis=1) * dmat).astype(BF)
            pr["main"] = _dot(lhs, jnp.concatenate([pr["top"], st.astype(BF)], axis=0))
            ktw = (pr["ktc"].astype(F32) * pr["w"]).astype(BF)
            dec = pr["dec"]
            state[pr["d"], pr["hd"]] = jnp.concatenate([dec, dec, dec], axis=1) * st + _dot(ktw, pr["top"])
        for pr in probs:
            main = pr["main"]
            r = 1.0 / jnp.maximum(jnp.abs(main[:, DV:]), pr["eneg"])
            h = main[:, 0:DV] * jnp.concatenate([r, r], axis=1)
            pr["h_ref"][0, pr["rows"], pr["hd"] * DV:(pr["hd"] + 1) * DV] = h.astype(BF)
        for (d, hd), st in state.items():
            st_ref[d, hd] = st
        return carry

    lax.fori_loop(0, cb // CHUNKS_PER_TRIP, body, 0)


def _scan(q, kt, v, prep, sb):
    bsz, s, _ = q.shape
    nb = s // sb
    cb = sb // CHUNK
    fwd3 = lambda b, j: (b, j, 0)
    bwd3 = lambda b, j: (b, nb - 1 - j, 0)
    in_specs = []
    for blk3, blkt, blkp in ((fwd3, lambda b, j: (b, 0, j), lambda b, j: (b, 0, j, 0, 0)),
                             (bwd3, lambda b, j: (b, 0, nb - 1 - j), lambda b, j: (b, 0, nb - 1 - j, 0, 0))):
        in_specs += [pl.BlockSpec((1, sb, QK_WIDTH), blk3), pl.BlockSpec((1, QK_WIDTH, sb), blkt),
                     pl.BlockSpec((1, sb, V_WIDTH), blk3), pl.BlockSpec((1, N_PREP, cb, 8, CHUNK), blkp)]
    return pl.pallas_call(
        _scan_body,
        grid=(bsz, nb),
        in_specs=in_specs,
        out_specs=[pl.BlockSpec((1, sb, V_WIDTH), fwd3), pl.BlockSpec((1, sb, V_WIDTH), bwd3)],
        out_shape=[jax.ShapeDtypeStruct((bsz, s, V_WIDTH), BF)] * 2,
        scratch_shapes=[pltpu.VMEM((2, HEADS, DK, DV + CHUNK), F32)],
        compiler_params=_params(("parallel", "arbitrary"), V7X_SCOPED_VMEM_BYTES),
        name="mlstm_scan",
    )(q, kt, v, prep, q, kt, v, prep)


def _mlp(xn, w1_ref, w2_ref, ff_chunk):
    d_ff = w1_ref.shape[2]
    acc = None
    for c0 in range(0, d_ff, ff_chunk):
        a = _dot(xn, w1_ref[0, :, c0:c0 + ff_chunk])
        a = jnp.square(jnp.maximum(a, 0.0)).astype(BF)
        part = _dot(a, w2_ref[0, c0:c0 + ff_chunk, :])
        acc = part if acc is None else acc + part
    return acc


def _mlp_and_embed(h, p_ref, nmlp_ref, w1_ref, w2_ref, nple_ref, gw_ref, gb_ref, plew_ref, ff_chunk):
    h = h + _mlp(_rmsnorm(h, nmlp_ref[...]).astype(BF), w1_ref, w2_ref, ff_chunk)
    gate = jax.nn.sigmoid(_dot(_rmsnorm(h, nple_ref[...]).astype(BF), gw_ref[0]) + gb_ref[...])
    return h + gate * _dot(p_ref[...].astype(BF), plew_ref[0])


def _tail0_body(hf_ref, hb_ref, so_ref, x_ref, p_ref, hnorm_ref, wout_ref, nmlp_ref, w1_ref, w2_ref, nple_ref,
                gw_ref, gb_ref, plew_ref, nmix_ref, pin_ref, h_ref, u_ref, *, ff_chunk):
    hs = hf_ref[...].astype(F32) + hb_ref[...].astype(F32)
    parts = []
    for hd in range(HEADS):
        blk = hs[:, hd * DV:(hd + 1) * DV]
        ms = jnp.mean(blk * blk, axis=-1, keepdims=True)
        parts.append(blk * lax.rsqrt(ms + EPS))
    hn = jnp.concatenate(parts, axis=1) * hnorm_ref[...] * so_ref[...].astype(F32)
    h = x_ref[...] + _dot(hn.astype(BF), wout_ref[...])
    h = _mlp_and_embed(h, p_ref, nmlp_ref, w1_ref, w2_ref, nple_ref, gw_ref, gb_ref, plew_ref, ff_chunk)
    h_ref[...] = h
    u_ref[...] = _dot(_rmsnorm(h, nmix_ref[...]).astype(BF), pin_ref[...])


def _tail0(hf, hb, so, x2, p2, weights, tm, ff_chunk):
    m, d = x2.shape
    return pl.pallas_call(
        functools.partial(_tail0_body, ff_chunk=ff_chunk),
        grid=(m // tm,),
        in_specs=[_rows(tm, d), _rows(tm, d), _rows(tm, d), _rows(tm, d), _rows(tm, p2.shape[1])]
                 + [spec for _, spec in weights],
        out_specs=[_rows(tm, d), _rows(tm, d)],
        out_shape=[jax.ShapeDtypeStruct((m, d), F32), jax.ShapeDtypeStruct((m, d), F32)],
        compiler_params=_params(("parallel",), V7X_SCOPED_VMEM_BYTES),
        name="layer0_tail",
    )(hf, hb, so, x2, p2, *[w for w, _ in weights])


def _layer1_body(u_ref, uprev_ref, unext_ref, h_ref, p_ref, wgrp_ref, scale_ref, pout_ref, nmlp_ref, w1_ref,
                 w2_ref, nple_ref, gw_ref, gb_ref, plew_ref, nfin_ref, out_ref, ext_ref, lvl_ref, *, seq, ff_chunk):
    tm, d = u_ref.shape
    gw = d // len(POOL_WINDOWS)
    rows = tm + 2 * POOL_HALO
    t0 = (pl.program_id(0) * tm) % seq
    ext_ref[0:POOL_HALO, :] = jnp.where(t0 > 0, uprev_ref[...], 0.0)
    ext_ref[POOL_HALO:POOL_HALO + tm, :] = u_ref[...]
    ext_ref[POOL_HALO + tm:rows, :] = jnp.where(t0 + tm < seq, unext_ref[...], 0.0)
    ext_ref[rows:, :] = jnp.zeros((POOL_HALO, d), F32)
    lvl_ref[:, rows:, :] = jnp.zeros((2, POOL_HALO, gw), F32)

    def window_count(first_row, win):
        t = first_row + lax.broadcasted_iota(jnp.int32, (POOL_HALO, gw), 0)
        return (jnp.minimum(t + (win - win // 2), seq) - jnp.maximum(t - win // 2, 0)).astype(F32)

    mixed = []
    for gi, win in enumerate(POOL_WINDOWS):
        cols = slice(gi * gw, (gi + 1) * gw)
        load = lambda off, n, cols=cols: ext_ref[pl.ds(off, n), cols]
        span, slot = 1, 0
        while 2 * span < win:
            lvl_ref[slot, 0:rows, :] = load(0, rows) + load(span, rows)
            load = lambda off, n, slot=slot: lvl_ref[slot, pl.ds(off, n), :]
            span, slot = 2 * span, 1 - slot
        start = POOL_HALO - win // 2
        total = load(start, tm) + load(start + span, tm)
        mean = jnp.concatenate([total[0:POOL_HALO] / window_count(t0, win),
                                total[POOL_HALO:tm - POOL_HALO] * (1.0 / win),
                                total[tm - POOL_HALO:] / window_count(t0 + tm - POOL_HALO, win)], axis=0)
        mixed.append(_dot((mean - u_ref[:, cols]).astype(BF), wgrp_ref[gi]))
    y = jnp.concatenate(mixed, axis=1) * scale_ref[...]
    h = h_ref[...] + _dot(y.astype(BF), pout_ref[...])
    h = _mlp_and_embed(h, p_ref, nmlp_ref, w1_ref, w2_ref, nple_ref, gw_ref, gb_ref, plew_ref, ff_chunk)
    out_ref[...] = _rmsnorm(h, nfin_ref[...])


def _layer1(u, h, p2, weights, tm, seq, ff_chunk):
    m, d = u.shape
    n_tiles = m // tm
    hpt = tm // POOL_HALO
    last_halo_block = m // POOL_HALO - 1
    gw = d // len(POOL_WINDOWS)
    return pl.pallas_call(
        functools.partial(_layer1_body, seq=seq, ff_chunk=ff_chunk),
        grid=(n_tiles,),
        in_specs=[_rows(tm, d),
                  pl.BlockSpec((POOL_HALO, d), lambda i: (jnp.maximum(i * hpt - 1, 0), 0)),
                  pl.BlockSpec((POOL_HALO, d), lambda i: (jnp.minimum((i + 1) * hpt, last_halo_block), 0)),
                  _rows(tm, d), _rows(tm, p2.shape[1], first_tile=n_tiles)]
                 + [spec for _, spec in weights],
        out_specs=_rows(tm, d),
        out_shape=jax.ShapeDtypeStruct((m, d), F32),
        scratch_shapes=[pltpu.VMEM((tm + 3 * POOL_HALO, d), F32), pltpu.VMEM((2, tm + 3 * POOL_HALO, gw), F32)],
        compiler_params=_params(("parallel",), V7X_SCOPED_VMEM_BYTES),
        name="layer1_pool_mlp",
    )(u, u, u, h, p2, *[w for w, _ in weights])


def kernel(x, p, norm_mix, norm_mlp, norm_ple, norm_final, mlstm_w_in, mlstm_b_gates, mlstm_head_norm,
           mlstm_w_out, pool_w_in, pool_w_grp, pool_scale, pool_w_out, mlp_w1, mlp_w2, ple_w, ple_gate_w,
           ple_gate_b):
    bsz, seq, d = x.shape
    m = bsz * seq
    nc = seq // CHUNK
    tm_proj, tm_fused, ff_chunk, scan_block = min(1024, seq), min(512, seq), 1024, min(2048, seq)
    assert seq % scan_block == 0 and seq % tm_fused == 0 and seq % tm_proj == 0
    assert scan_block % (CHUNK * CHUNKS_PER_TRIP) == 0
    row = lambda v: v.reshape(1, -1)
    bf = lambda w: w.astype(BF)

    def whole(v):
        return v, _resident(v.shape)

    def layer(stacked, i):
        return stacked, _layer_of(stacked, i)

    x2 = x.reshape(m, d)
    p2 = p.reshape(-1, p.shape[-1])
    n_main = 2 * QK_WIDTH + 2 * V_WIDTH
    w_in = bf(mlstm_w_in)
    w_kt = w_in[0, :, QK_WIDTH:2 * QK_WIDTH].T
    gate_perm = jnp.array([0, 1, 2, 3, 8, 9, 10, 11, 4, 5, 6, 7, 12, 13, 14, 15], jnp.int32)
    w_gt = w_in[0, :, n_main:].T[gate_perm]
    bias_rows = jnp.broadcast_to(mlstm_b_gates[0].reshape(N_GATES)[gate_perm][:, None], (N_GATES, CHUNK))

    q, kt, v, so, gt = _inproj(x2, row(norm_mix[0]), w_in, w_kt, w_gt, tm_proj, seq)
    prep = _prep(gt, bias_rows)
    hf, hb = _scan(q.reshape(bsz, seq, QK_WIDTH), kt, v.reshape(bsz, seq, V_WIDTH), prep, scan_block)

    w1, w2, gate_w, emb_w = bf(mlp_w1), bf(mlp_w2), bf(ple_gate_w), bf(ple_w)

    def mlp_embed_weights(i):
        return [whole(row(norm_mlp[i])), layer(w1, i), layer(w2, i), whole(row(norm_ple[i])), layer(gate_w, i),
                whole(row(ple_gate_b[i])), layer(emb_w, i)]

    tail_weights = ([whole(row(mlstm_head_norm[0])), whole(bf(mlstm_w_out[0]))] + mlp_embed_weights(0)
                    + [whole(row(norm_mix[1])), whole(bf(pool_w_in[0]))])
    h, u = _tail0(hf.reshape(m, V_WIDTH), hb.reshape(m, V_WIDTH), so, x2, p2, tail_weights, tm_fused, ff_chunk)

    l1_weights = ([whole(bf(pool_w_grp[0])), whole(row(pool_scale[0])), whole(bf(pool_w_out[0]))]
                  + mlp_embed_weights(1) + [whole(row(norm_final))])
    out = _layer1(u, h, p2, l1_weights, tm_fused, seq, ff_chunk)
    return out.reshape(bsz, seq, d)
```

```python
import functools

import jax
import jax.numpy as jnp
from jax import lax
from jax.experimental import pallas as pl
from jax.experimental.pallas import tpu as pltpu

EPS = 1e-6
HEADS = 4
DK = 128
DV = 256
CHUNK = 128
QK_WIDTH = HEADS * DK
V_WIDTH = HEADS * DV
N_GATES = 4 * HEADS
POOL_WINDOWS = (2, 4, 8, 16)
POOL_HALO = 8
N_PROBLEMS = 2 * HEADS
N_PREP = 6
CHUNKS_PER_TRIP = 2

LANES = 128
V7X_SCOPED_VMEM_BYTES = 60000 * 1024

LOG2E = 1.4426950408889634

BF = jnp.bfloat16
F32 = jnp.float32


def _dot(a, b):
    return jnp.dot(a, b, preferred_element_type=F32)


def _rmsnorm(x, gain):
    ms = jnp.mean(x * x, axis=-1, keepdims=True)
    return x * lax.rsqrt(ms + EPS) * gain


def _resident(shape):
    nd = len(shape)
    return pl.BlockSpec(shape, lambda *_: (0,) * nd, pipeline_mode=pl.Buffered(1))


def _layer_of(stacked, layer):
    nd = stacked.ndim
    return pl.BlockSpec((1,) + stacked.shape[1:], lambda *_: (layer,) + (0,) * (nd - 1),
                        pipeline_mode=pl.Buffered(1))


def _rows(tm, width, first_tile=0):
    return pl.BlockSpec((tm, width), lambda i: (i + first_tile, 0))


def _params(sem, vmem=None):
    return pltpu.CompilerParams(dimension_semantics=sem, vmem_limit_bytes=vmem)


def _inproj_body(x_ref, gain_ref, w_ref, wkt_ref, wgt_ref, q_ref, kt_ref, v_ref, so_ref, gt_ref):
    v0 = 2 * QK_WIDTH
    nt = (((1,), (1,)), ((), ()))
    xn = _rmsnorm(x_ref[...], gain_ref[...]).astype(BF)
    q_ref[...] = _dot(xn, w_ref[0, :, 0:QK_WIDTH]).astype(BF)
    kt = lax.dot_general(wkt_ref[...], xn, nt, preferred_element_type=F32)
    kt_ref[0] = (kt * (DK ** -0.5)).astype(BF)
    v_ref[...] = _dot(xn, w_ref[0, :, v0:v0 + V_WIDTH]).astype(BF)
    so_ref[...] = jax.nn.sigmoid(_dot(xn, w_ref[0, :, v0 + V_WIDTH:v0 + 2 * V_WIDTH])).astype(BF)
    gt = lax.dot_general(wgt_ref[...], xn, nt, preferred_element_type=F32)
    for c in range(gt_ref.shape[1]):
        gt_ref[0, c] = gt[:, c * CHUNK:(c + 1) * CHUNK]


def _inproj(x2, gain, w_in, w_kt, w_gt, tm, seq):
    m, d = x2.shape
    tiles_per_seq = seq // tm
    cpt = tm // CHUNK
    per_seq = lambda i: (i // tiles_per_seq, i % tiles_per_seq)
    return pl.pallas_call(
        _inproj_body,
        grid=(m // tm,),
        in_specs=[_rows(tm, d), _resident(gain.shape), _layer_of(w_in, 0), _resident(w_kt.shape),
                  _resident(w_gt.shape)],
        out_specs=[_rows(tm, QK_WIDTH),
                   pl.BlockSpec((1, QK_WIDTH, tm), lambda i: (per_seq(i)[0], 0, per_seq(i)[1])),
                   _rows(tm, V_WIDTH), _rows(tm, V_WIDTH),
                   pl.BlockSpec((1, cpt, N_GATES, CHUNK), lambda i: (per_seq(i)[0], per_seq(i)[1], 0, 0))],
        out_shape=[jax.ShapeDtypeStruct((m, QK_WIDTH), BF), jax.ShapeDtypeStruct((m // seq, QK_WIDTH, seq), BF),
                   jax.ShapeDtypeStruct((m, V_WIDTH), BF), jax.ShapeDtypeStruct((m, V_WIDTH), BF),
                   jax.ShapeDtypeStruct((m // seq, seq // CHUNK, N_GATES, CHUNK), F32)],
        compiler_params=_params(("parallel",), V7X_SCOPED_VMEM_BYTES),
        name="mlstm_inproj",
    )(x2, gain, w_in, w_kt, w_gt)


def _log_sigmoid(x):
    return jnp.minimum(x, 0.0) - jnp.log1p(jnp.exp(-jnp.abs(x)))


def _split3(x):
    hi = x.astype(BF)
    r1 = x - hi.astype(F32)
    mid = r1.astype(BF)
    lo = (r1 - mid.astype(F32)).astype(BF)
    return hi, mid, lo


def _prep_body(gt_ref, bias_ref, out_ref, mprev_ref):
    nc = gt_ref.shape[1]
    L = CHUNK
    g = gt_ref[0] + bias_ref[...][None]
    i_pre = g[:, 0:N_PROBLEMS, :].reshape(nc * N_PROBLEMS, L)
    logf = _log_sigmoid(g[:, N_PROBLEMS:2 * N_PROBLEMS, :]).reshape(nc * N_PROBLEMS, L)

    row = lax.broadcasted_iota(jnp.int32, (nc * N_PROBLEMS, L), 0)
    lane = lax.broadcasted_iota(jnp.int32, (nc * N_PROBLEMS, L), 1)
    is_fwd = (row % N_PROBLEMS) < HEADS

    s_idx = lax.broadcasted_iota(jnp.int32, (L, L), 0)
    t_idx = lax.broadcasted_iota(jnp.int32, (L, L), 1)
    tri_pre = (s_idx <= t_idx).astype(BF)
    tri_suf = (s_idx >= t_idx).astype(BF)
    pre = jnp.zeros((nc * N_PROBLEMS, L), F32)
    suf = jnp.zeros((nc * N_PROBLEMS, L), F32)
    for piece in _split3(logf):
        pre = pre + _dot(piece, tri_pre)
        suf = suf + _dot(piece, tri_suf)
    b = jnp.where(is_fwd, pre, suf)
    a = i_pre - b

    pm = a
    sm = a
    k = 1
    while k < L:
        pm = jnp.where(lane >= k, jnp.maximum(pm, pltpu.roll(pm, k, 1)), pm)
        sm = jnp.where(lane < L - k, jnp.maximum(sm, pltpu.roll(sm, L - k, 1)), sm)
        k *= 2
    cm = jnp.where(is_fwd, pm, sm)

    amax = jnp.broadcast_to(jnp.max(a, axis=1, keepdims=True), (nc * N_PROBLEMS, L)).reshape(nc, N_PROBLEMS, L)
    b_last = jnp.broadcast_to(jnp.sum(logf, axis=1, keepdims=True), (nc * N_PROBLEMS, L)).reshape(nc, N_PROBLEMS, L)

    fwd_rows = lax.broadcasted_iota(jnp.int32, (N_PROBLEMS, L), 0) < HEADS
    m = jnp.zeros((N_PROBLEMS, L), F32)
    for c in range(nc):
        cb = nc - 1 - c
        mprev_ref[c, 0:HEADS, :] = m[0:HEADS]
        mprev_ref[cb, HEADS:, :] = m[HEADS:]
        am = jnp.where(fwd_rows, amax[c], amax[cb])
        bl = jnp.where(fwd_rows, b_last[c], b_last[cb])
        m = bl + jnp.maximum(m, am)

    mprev = mprev_ref[...]
    a3 = a.reshape(nc, N_PROBLEMS, L)
    b3 = b.reshape(nc, N_PROBLEMS, L)
    cm3 = cm.reshape(nc, N_PROBLEMS, L)
    sigma = jnp.maximum(mprev, amax)
    mm = jnp.maximum(mprev, cm3)
    out_ref[0, 0] = a3 * LOG2E
    out_ref[0, 1] = mprev * LOG2E
    out_ref[0, 2] = mm * LOG2E
    out_ref[0, 3] = jnp.exp(-(b3 + mm))
    out_ref[0, 4] = jnp.exp(a3 - sigma)
    out_ref[0, 5] = jnp.exp(mprev - sigma)


def _prep(gt, bias_rows):
    bsz, nc, _, L = gt.shape
    return pl.pallas_call(
        _prep_body,
        grid=(bsz,),
        in_specs=[pl.BlockSpec((1, nc, N_GATES, L), lambda b: (b, 0, 0, 0)), _resident(bias_rows.shape)],
        out_specs=pl.BlockSpec((1, N_PREP, nc, N_PROBLEMS, L), lambda b: (b, 0, 0, 0, 0)),
        out_shape=jax.ShapeDtypeStruct((bsz, N_PREP, nc, N_PROBLEMS, L), F32),
        scratch_shapes=[pltpu.VMEM((nc, N_PROBLEMS, L), F32)],
        compiler_params=_params(("parallel",)),
        name="mlstm_gate_prep",
    )(gt, bias_rows)


def _scan_body(qf_ref, ktf_ref, vf_ref, pf_ref, qb_ref, ktb_ref, vb_ref, pb_ref, hf_ref, hb_ref, st_ref):
    L = CHUNK
    cb = pf_ref.shape[2]

    @pl.when(pl.program_id(1) == 0)
    def _():
        st_ref[...] = jnp.zeros(st_ref.shape, F32)

    t_idx = lax.broadcasted_iota(jnp.int32, (L, 2 * L), 0)
    s_idx = lax.broadcasted_iota(jnp.int32, (L, 2 * L), 1)
    masks = ((s_idx <= t_idx) | (s_idx >= L), (s_idx >= t_idx))
    ones_blk = jnp.ones((L, L), BF)

    dirs = ((qf_ref, ktf_ref, vf_ref, pf_ref, hf_ref), (qb_ref, ktb_ref, vb_ref, pb_ref, hb_ref))

    def body(ci, carry):
        probs = []
        for sub in range(CHUNKS_PER_TRIP):
            for d, (q_ref, kt_ref, v_ref, p_ref, h_ref) in enumerate(dirs):
                step = ci * CHUNKS_PER_TRIP + sub
                c = step if d == 0 else cb - 1 - step
                rows = pl.ds(pl.multiple_of(c * L, L), L)
                a2, mprev2, mm2, eneg, w, decay = [p_ref[0, qi, c] for qi in range(N_PREP)]
                mm2_cols = mm2.T
                eneg_cols = eneg.T
                for hd in range(HEADS):
                    rix = d * HEADS + hd
                    qc = q_ref[0, rows, hd * DK:(hd + 1) * DK]
                    ktc = kt_ref[0, hd * DK:(hd + 1) * DK, rows]
                    vc = v_ref[0, rows, hd * DV:(hd + 1) * DV]
                    probs.append(dict(
                        d=d, hd=hd, rows=rows, h_ref=h_ref, qc=qc, ktc=ktc, s=_dot(qc, ktc),
                        top=jnp.concatenate([vc, ones_blk], axis=1),
                        rowvec=jnp.concatenate([a2[rix:rix + 1], mprev2[rix:rix + 1]], axis=1),
                        mm2=mm2_cols[:, rix:rix + 1], eneg=eneg_cols[:, rix:rix + 1],
                        w=w[rix:rix + 1], dec=decay[rix:rix + 1]))
        state = {(d, hd): st_ref[d, hd] for d in range(2) for hd in range(HEADS)}
        for pr in probs:
            st = state[pr["d"], pr["hd"]]
            dmat = jnp.where(masks[pr["d"]], jnp.exp2(pr["rowvec"] - pr["mm2"]), 0.0)
            lhs = (jnp.concatenate([pr["s"], pr["qc"].astype(F32)], axis=1) * dmat).astype(BF)
            pr["main"] = _dot(lhs, jnp.concatenate([pr["top"], st.astype(BF)], axis=0))
            ktw = (pr["ktc"].astype(F32) * pr["w"]).astype(BF)
            dec = pr["dec"]
            state[pr["d"], pr["hd"]] = jnp.concatenate([dec, dec, dec], axis=1) * st + _dot(ktw, pr["top"])
        for pr in probs:
            main = pr["main"]
            r = 1.0 / jnp.maximum(jnp.abs(main[:, DV:]), pr["eneg"])
            h = main[:, 0:DV] * jnp.concatenate([r, r], axis=1)
            pr["h_ref"][0, pr["rows"], pr["hd"] * DV:(pr["hd"] + 1) * DV] = h.astype(BF)
        for (d, hd), st in state.items():
            st_ref[d, hd] = st
        return carry

    lax.fori_loop(0, cb // CHUNKS_PER_TRIP, body, 0)


def _scan(q, kt, v, prep, sb):
    bsz, s, _ = q.shape
    nb = s // sb
    cb = sb // CHUNK
    fwd3 = lambda b, j: (b, j, 0)
    bwd3 = lambda b, j: (b, nb - 1 - j, 0)
    in_specs = []
    for blk3, blkt, blkp in ((fwd3, lambda b, j: (b, 0, j), lambda b, j: (b, 0, j, 0, 0)),
                             (bwd3, lambda b, j: (b, 0, nb - 1 - j), lambda b, j: (b, 0, nb - 1 - j, 0, 0))):
        in_specs += [pl.BlockSpec((1, sb, QK_WIDTH), blk3), pl.BlockSpec((1, QK_WIDTH, sb), blkt),
                     pl.BlockSpec((1, sb, V_WIDTH), blk3), pl.BlockSpec((1, N_PREP, cb, N_PROBLEMS, CHUNK), blkp)]
    return pl.pallas_call(
        _scan_body,
        grid=(bsz, nb),
        in_specs=in_specs,
        out_specs=[pl.BlockSpec((1, sb, V_WIDTH), fwd3), pl.BlockSpec((1, sb, V_WIDTH), bwd3)],
        out_shape=[jax.ShapeDtypeStruct((bsz, s, V_WIDTH), BF)] * 2,
        scratch_shapes=[pltpu.VMEM((2, HEADS, DK, DV + CHUNK), F32)],
        compiler_params=_params(("parallel", "arbitrary"), V7X_SCOPED_VMEM_BYTES),
        name="mlstm_scan",
    )(q, kt, v, prep, q, kt, v, prep)


def _mlp(xn, w1_ref, w2_ref, ff_chunk):
    d_ff = w1_ref.shape[2]
    acc = None
    for c0 in range(0, d_ff, ff_chunk):
        a = _dot(xn, w1_ref[0, :, c0:c0 + ff_chunk])
        a = jnp.square(jnp.maximum(a, 0.0)).astype(BF)
        part = _dot(a, w2_ref[0, c0:c0 + ff_chunk, :])
        acc = part if acc is None else acc + part
    return acc


def _mlp_and_embed(h, p_ref, nmlp_ref, w1_ref, w2_ref, nple_ref, gw_ref, gb_ref, plew_ref, ff_chunk):
    h = h + _mlp(_rmsnorm(h, nmlp_ref[...]).astype(BF), w1_ref, w2_ref, ff_chunk)
    gate = jax.nn.sigmoid(_dot(_rmsnorm(h, nple_ref[...]).astype(BF), gw_ref[0]) + gb_ref[...])
    return h + gate * _dot(p_ref[...].astype(BF), plew_ref[0])


def _tail0_body(hf_ref, hb_ref, so_ref, x_ref, p_ref, hnorm_ref, wout_ref, nmlp_ref, w1_ref, w2_ref, nple_ref,
                gw_ref, gb_ref, plew_ref, nmix_ref, pin_ref, h_ref, u_ref, *, ff_chunk):
    hs = hf_ref[...].astype(F32) + hb_ref[...].astype(F32)
    parts = []
    for hd in range(HEADS):
        blk = hs[:, hd * DV:(hd + 1) * DV]
        ms = jnp.mean(blk * blk, axis=-1, keepdims=True)
        parts.append(blk * lax.rsqrt(ms + EPS))
    hn = jnp.concatenate(parts, axis=1) * hnorm_ref[...] * so_ref[...].astype(F32)
    h = x_ref[...] + _dot(hn.astype(BF), wout_ref[...])
    h = _mlp_and_embed(h, p_ref, nmlp_ref, w1_ref, w2_ref, nple_ref, gw_ref, gb_ref, plew_ref, ff_chunk)
    h_ref[...] = h
    u_ref[...] = _dot(_rmsnorm(h, nmix_ref[...]).astype(BF), pin_ref[...])


def _tail0(hf, hb, so, x2, p2, weights, tm, ff_chunk):
    m, d = x2.shape
    return pl.pallas_call(
        functools.partial(_tail0_body, ff_chunk=ff_chunk),
        grid=(m // tm,),
        in_specs=[_rows(tm, d), _rows(tm, d), _rows(tm, d), _rows(tm, d), _rows(tm, p2.shape[1])]
                 + [spec for _, spec in weights],
        out_specs=[_rows(tm, d), _rows(tm, d)],
        out_shape=[jax.ShapeDtypeStruct((m, d), F32), jax.ShapeDtypeStruct((m, d), F32)],
        compiler_params=_params(("parallel",), V7X_SCOPED_VMEM_BYTES),
        name="layer0_tail",
    )(hf, hb, so, x2, p2, *[w for w, _ in weights])


def _layer1_body(u_ref, uprev_ref, unext_ref, h_ref, p_ref, wgrp_ref, scale_ref, pout_ref, nmlp_ref, w1_ref,
                 w2_ref, nple_ref, gw_ref, gb_ref, plew_ref, nfin_ref, out_ref, ext_ref, lvl_ref, *, seq, ff_chunk):
    tm, d = u_ref.shape
    gw = d // len(POOL_WINDOWS)
    rows = tm + 2 * POOL_HALO
    t0 = (pl.program_id(0) * tm) % seq
    ext_ref[0:POOL_HALO, :] = jnp.where(t0 > 0, uprev_ref[...], 0.0)
    ext_ref[POOL_HALO:POOL_HALO + tm, :] = u_ref[...]
    ext_ref[POOL_HALO + tm:rows, :] = jnp.where(t0 + tm < seq, unext_ref[...], 0.0)
    ext_ref[rows:, :] = jnp.zeros((POOL_HALO, d), F32)
    lvl_ref[:, rows:, :] = jnp.zeros((2, POOL_HALO, gw), F32)

    def window_count(first_row, win):
        t = first_row + lax.broadcasted_iota(jnp.int32, (POOL_HALO, gw), 0)
        return (jnp.minimum(t + (win - win // 2), seq) - jnp.maximum(t - win // 2, 0)).astype(F32)

    mixed = []
    for gi, win in enumerate(POOL_WINDOWS):
        cols = slice(gi * gw, (gi + 1) * gw)
        load = lambda off, n, cols=cols: ext_ref[pl.ds(off, n), cols]
        span, slot = 1, 0
        while 2 * span < win:
            lvl_ref[slot, 0:rows, :] = load(0, rows) + load(span, rows)
            load = lambda off, n, slot=slot: lvl_ref[slot, pl.ds(off, n), :]
            span, slot = 2 * span, 1 - slot
        start = POOL_HALO - win // 2
        total = load(start, tm) + load(start + span, tm)
        mean = jnp.concatenate([total[0:POOL_HALO] / window_count(t0, win),
                                total[POOL_HALO:tm - POOL_HALO] * (1.0 / win),
                                total[tm - POOL_HALO:] / window_count(t0 + tm - POOL_HALO, win)], axis=0)
        mixed.append(_dot((mean - u_ref[:, cols]).astype(BF), wgrp_ref[gi]))
    y = jnp.concatenate(mixed, axis=1) * scale_ref[...]
    h = h_ref[...] + _dot(y.astype(BF), pout_ref[...])
    h = _mlp_and_embed(h, p_ref, nmlp_ref, w1_ref, w2_ref, nple_ref, gw_ref, gb_ref, plew_ref, ff_chunk)
    out_ref[...] = _rmsnorm(h, nfin_ref[...])


def _layer1(u, h, p2, weights, tm, seq, ff_chunk):
    m, d = u.shape
    n_tiles = m // tm
    hpt = tm // POOL_HALO
    last_halo_block = m // POOL_HALO - 1
    gw = d // len(POOL_WINDOWS)
    return pl.pallas_call(
        functools.partial(_layer1_body, seq=seq, ff_chunk=ff_chunk),
        grid=(n_tiles,),
        in_specs=[_rows(tm, d),
                  pl.BlockSpec((POOL_HALO, d), lambda i: (jnp.maximum(i * hpt - 1, 0), 0)),
                  pl.BlockSpec((POOL_HALO, d), lambda i: (jnp.minimum((i + 1) * hpt, last_halo_block), 0)),
                  _rows(tm, d), _rows(tm, p2.shape[1], first_tile=n_tiles)]
                 + [spec for _, spec in weights],
        out_specs=_rows(tm, d),
        out_shape=jax.ShapeDtypeStruct((m, d), F32),
        scratch_shapes=[pltpu.VMEM((tm + 3 * POOL_HALO, d), F32), pltpu.VMEM((2, tm + 3 * POOL_HALO, gw), F32)],
        compiler_params=_params(("parallel",), V7X_SCOPED_VMEM_BYTES),
        name="layer1_pool_mlp",
    )(u, u, u, h, p2, *[w for w, _ in weights])


def _tile_plan(seq):
    tm_proj, tm_fused, ff_chunk, scan_block = min(1024, seq), min(512, seq), 1024, min(2048, seq)
    assert seq % scan_block == 0 and seq % tm_fused == 0 and seq % tm_proj == 0
    assert scan_block % (CHUNK * CHUNKS_PER_TRIP) == 0 and tm_proj % CHUNK == 0 and tm_fused >= 2 * POOL_HALO
    return tm_proj, tm_fused, ff_chunk, scan_block


def kernel(x, p, norm_mix, norm_mlp, norm_ple, norm_final, mlstm_w_in, mlstm_b_gates, mlstm_head_norm,
           mlstm_w_out, pool_w_in, pool_w_grp, pool_scale, pool_w_out, mlp_w1, mlp_w2, ple_w, ple_gate_w,
           ple_gate_b):
    bsz, seq, d = x.shape
    m = bsz * seq
    tm_proj, tm_fused, ff_chunk, scan_block = _tile_plan(seq)
    row = lambda v: v.reshape(1, -1)
    bf = lambda w: w.astype(BF)

    def whole(v):
        return v, _resident(v.shape)

    def layer(stacked, i):
        return stacked, _layer_of(stacked, i)

    x2 = x.reshape(m, d)
    p2 = p.reshape(-1, p.shape[-1])
    n_main = 2 * QK_WIDTH + 2 * V_WIDTH
    w_in = bf(mlstm_w_in)
    w_kt = w_in[0, :, QK_WIDTH:2 * QK_WIDTH].T
    gate_perm = jnp.array([0, 1, 2, 3, 8, 9, 10, 11, 4, 5, 6, 7, 12, 13, 14, 15], jnp.int32)
    w_gt = w_in[0, :, n_main:].T[gate_perm]
    bias_rows = jnp.broadcast_to(mlstm_b_gates[0].reshape(N_GATES)[gate_perm][:, None], (N_GATES, CHUNK))

    q, kt, v, so, gt = _inproj(x2, row(norm_mix[0]), w_in, w_kt, w_gt, tm_proj, seq)
    prep = _prep(gt, bias_rows)
    hf, hb = _scan(q.reshape(bsz, seq, QK_WIDTH), kt, v.reshape(bsz, seq, V_WIDTH), prep, scan_block)

    w1, w2, gate_w, emb_w = bf(mlp_w1), bf(mlp_w2), bf(ple_gate_w), bf(ple_w)

    def mlp_embed_weights(i):
        return [whole(row(norm_mlp[i])), layer(w1, i), layer(w2, i), whole(row(norm_ple[i])), layer(gate_w, i),
                whole(row(ple_gate_b[i])), layer(emb_w, i)]

    tail_weights = ([whole(row(mlstm_head_norm[0])), whole(bf(mlstm_w_out[0]))] + mlp_embed_weights(0)
                    + [whole(row(norm_mix[1])), whole(bf(pool_w_in[0]))])
    h, u = _tail0(hf.reshape(m, V_WIDTH), hb.reshape(m, V_WIDTH), so, x2, p2, tail_weights, tm_fused, ff_chunk)

    l1_weights = ([whole(bf(pool_w_grp[0])), whole(row(pool_scale[0])), whole(bf(pool_w_out[0]))]
                  + mlp_embed_weights(1) + [whole(row(norm_final))])
    out = _layer1(u, h, p2, l1_weights, tm_fused, seq, ff_chunk)
    return out.reshape(bsz, seq, d)
```

```python
import functools

import jax
import jax.numpy as jnp
from jax import lax
from jax.experimental import pallas as pl
from jax.experimental.pallas import tpu as pltpu

EPS = 1e-6
HEADS = 4
DK = 128
DV = 256
CHUNK = 128
QK_WIDTH = HEADS * DK
V_WIDTH = HEADS * DV
N_GATES = 4 * HEADS
POOL_WINDOWS = (2, 4, 8, 16)
POOL_HALO = 8
N_PROBLEMS = 2 * HEADS
N_PREP = 6
CHUNKS_PER_TRIP = 2

LANES = 128
BF16_TILE_ROWS = 16
V7X_SCOPED_VMEM_BYTES = 60000 * 1024

LOG2E = 1.4426950408889634

BF = jnp.bfloat16
F32 = jnp.float32


def _dot(a, b):
    return jnp.dot(a, b, preferred_element_type=F32)


def _rmsnorm(x, gain):
    ms = jnp.mean(x * x, axis=-1, keepdims=True)
    return x * lax.rsqrt(ms + EPS) * gain


def _resident(shape):
    nd = len(shape)
    return pl.BlockSpec(shape, lambda *_: (0,) * nd, pipeline_mode=pl.Buffered(1))


def _layer_of(stacked, layer):
    nd = stacked.ndim
    return pl.BlockSpec((1,) + stacked.shape[1:], lambda *_: (layer,) + (0,) * (nd - 1),
                        pipeline_mode=pl.Buffered(1))


def _rows(tm, width, first_tile=0):
    return pl.BlockSpec((tm, width), lambda i: (i + first_tile, 0))


def _params(sem, vmem=None):
    return pltpu.CompilerParams(dimension_semantics=sem, vmem_limit_bytes=vmem)


def _inproj_body(x_ref, gain_ref, w_ref, wkt_ref, wgt_ref, *refs, n_cast):
    cast_src, (q_ref, kt_ref, v_ref, so_ref, gt_ref), cast_dst = refs[:n_cast], refs[n_cast:n_cast + 5], refs[n_cast + 5:]
    for src, dst in zip(cast_src, cast_dst):
        dst[...] = src[...].astype(BF)
    v0 = 2 * QK_WIDTH
    nt = (((1,), (1,)), ((), ()))
    xn = _rmsnorm(x_ref[...], gain_ref[...]).astype(BF)
    q_ref[...] = _dot(xn, w_ref[0, :, 0:QK_WIDTH]).astype(BF)
    kt = lax.dot_general(wkt_ref[...], xn, nt, preferred_element_type=F32)
    kt_ref[0] = (kt * (DK ** -0.5)).astype(BF)
    v_ref[...] = _dot(xn, w_ref[0, :, v0:v0 + V_WIDTH]).astype(BF)
    so_ref[...] = jax.nn.sigmoid(_dot(xn, w_ref[0, :, v0 + V_WIDTH:v0 + 2 * V_WIDTH])).astype(BF)
    gt = lax.dot_general(wgt_ref[...], xn, nt, preferred_element_type=F32)
    for c in range(gt_ref.shape[1]):
        gt_ref[0, c] = gt[:, c * CHUNK:(c + 1) * CHUNK]


def _inproj(x2, gain, w_in, w_kt, w_gt, to_cast, tm, seq):
    m, d = x2.shape
    steps = m // tm
    tiles_per_seq = seq // tm
    cpt = tm // CHUNK
    per_seq = lambda i: (i // tiles_per_seq, i % tiles_per_seq)
    slabs = [_rows(w.shape[0] // steps, w.shape[1]) for w in to_cast]
    out = pl.pallas_call(
        functools.partial(_inproj_body, n_cast=len(to_cast)),
        grid=(steps,),
        in_specs=[_rows(tm, d), _resident(gain.shape), _layer_of(w_in, 0), _resident(w_kt.shape),
                  _resident(w_gt.shape)] + slabs,
        out_specs=[_rows(tm, QK_WIDTH),
                   pl.BlockSpec((1, QK_WIDTH, tm), lambda i: (per_seq(i)[0], 0, per_seq(i)[1])),
                   _rows(tm, V_WIDTH), _rows(tm, V_WIDTH),
                   pl.BlockSpec((1, cpt, N_GATES, CHUNK), lambda i: (per_seq(i)[0], per_seq(i)[1], 0, 0))] + slabs,
        out_shape=[jax.ShapeDtypeStruct((m, QK_WIDTH), BF), jax.ShapeDtypeStruct((m // seq, QK_WIDTH, seq), BF),
                   jax.ShapeDtypeStruct((m, V_WIDTH), BF), jax.ShapeDtypeStruct((m, V_WIDTH), BF),
                   jax.ShapeDtypeStruct((m // seq, seq // CHUNK, N_GATES, CHUNK), F32)]
                  + [jax.ShapeDtypeStruct(w.shape, BF) for w in to_cast],
        compiler_params=_params(("parallel",), V7X_SCOPED_VMEM_BYTES),
        name="mlstm_inproj",
    )(x2, gain, w_in, w_kt, w_gt, *to_cast)
    return out[:5], out[5:]


def _log_sigmoid(x):
    return jnp.minimum(x, 0.0) - jnp.log1p(jnp.exp(-jnp.abs(x)))


def _split3(x):
    hi = x.astype(BF)
    r1 = x - hi.astype(F32)
    mid = r1.astype(BF)
    lo = (r1 - mid.astype(F32)).astype(BF)
    return hi, mid, lo


def _prep_body(gt_ref, bias_ref, out_ref, mprev_ref):
    nc = gt_ref.shape[1]
    L = CHUNK
    g = gt_ref[0] + bias_ref[...][None]
    i_pre = g[:, 0:N_PROBLEMS, :].reshape(nc * N_PROBLEMS, L)
    logf = _log_sigmoid(g[:, N_PROBLEMS:2 * N_PROBLEMS, :]).reshape(nc * N_PROBLEMS, L)

    row = lax.broadcasted_iota(jnp.int32, (nc * N_PROBLEMS, L), 0)
    lane = lax.broadcasted_iota(jnp.int32, (nc * N_PROBLEMS, L), 1)
    is_fwd = (row % N_PROBLEMS) < HEADS

    s_idx = lax.broadcasted_iota(jnp.int32, (L, L), 0)
    t_idx = lax.broadcasted_iota(jnp.int32, (L, L), 1)
    tri_pre = (s_idx <= t_idx).astype(BF)
    tri_suf = (s_idx >= t_idx).astype(BF)
    pre = jnp.zeros((nc * N_PROBLEMS, L), F32)
    suf = jnp.zeros((nc * N_PROBLEMS, L), F32)
    for piece in _split3(logf):
        pre = pre + _dot(piece, tri_pre)
        suf = suf + _dot(piece, tri_suf)
    b = jnp.where(is_fwd, pre, suf)
    a = i_pre - b

    pm = a
    sm = a
    k = 1
    while k < L:
        pm = jnp.where(lane >= k, jnp.maximum(pm, pltpu.roll(pm, k, 1)), pm)
        sm = jnp.where(lane < L - k, jnp.maximum(sm, pltpu.roll(sm, L - k, 1)), sm)
        k *= 2
    cm = jnp.where(is_fwd, pm, sm)

    amax = jnp.broadcast_to(jnp.max(a, axis=1, keepdims=True), (nc * N_PROBLEMS, L)).reshape(nc, N_PROBLEMS, L)
    b_last = jnp.broadcast_to(jnp.sum(logf, axis=1, keepdims=True), (nc * N_PROBLEMS, L)).reshape(nc, N_PROBLEMS, L)

    fwd_rows = lax.broadcasted_iota(jnp.int32, (N_PROBLEMS, L), 0) < HEADS
    m = jnp.zeros((N_PROBLEMS, L), F32)
    for c in range(nc):
        cb = nc - 1 - c
        mprev_ref[c, 0:HEADS, :] = m[0:HEADS]
        mprev_ref[cb, HEADS:, :] = m[HEADS:]
        am = jnp.where(fwd_rows, amax[c], amax[cb])
        bl = jnp.where(fwd_rows, b_last[c], b_last[cb])
        m = bl + jnp.maximum(m, am)

    mprev = mprev_ref[...]
    a3 = a.reshape(nc, N_PROBLEMS, L)
    b3 = b.reshape(nc, N_PROBLEMS, L)
    cm3 = cm.reshape(nc, N_PROBLEMS, L)
    sigma = jnp.maximum(mprev, amax)
    mm = jnp.maximum(mprev, cm3)
    out_ref[0, 0] = a3 * LOG2E
    out_ref[0, 1] = mprev * LOG2E
    out_ref[0, 2] = mm * LOG2E
    out_ref[0, 3] = jnp.exp(-(b3 + mm))
    out_ref[0, 4] = jnp.exp(a3 - sigma)
    out_ref[0, 5] = jnp.exp(mprev - sigma)


def _prep(gt, bias_rows):
    bsz, nc, _, L = gt.shape
    return pl.pallas_call(
        _prep_body,
        grid=(bsz,),
        in_specs=[pl.BlockSpec((1, nc, N_GATES, L), lambda b: (b, 0, 0, 0)), _resident(bias_rows.shape)],
        out_specs=pl.BlockSpec((1, N_PREP, nc, N_PROBLEMS, L), lambda b: (b, 0, 0, 0, 0)),
        out_shape=jax.ShapeDtypeStruct((bsz, N_PREP, nc, N_PROBLEMS, L), F32),
        scratch_shapes=[pltpu.VMEM((nc, N_PROBLEMS, L), F32)],
        compiler_params=_params(("parallel",)),
        name="mlstm_gate_prep",
    )(gt, bias_rows)


def _scan_body(qf_ref, ktf_ref, vf_ref, pf_ref, qb_ref, ktb_ref, vb_ref, pb_ref, hf_ref, hb_ref, st_ref):
    L = CHUNK
    cb = pf_ref.shape[2]

    @pl.when(pl.program_id(1) == 0)
    def _():
        st_ref[...] = jnp.zeros(st_ref.shape, F32)

    t_idx = lax.broadcasted_iota(jnp.int32, (L, 2 * L), 0)
    s_idx = lax.broadcasted_iota(jnp.int32, (L, 2 * L), 1)
    masks = ((s_idx <= t_idx) | (s_idx >= L), (s_idx >= t_idx))
    ones_blk = jnp.ones((L, L), BF)

    dirs = ((qf_ref, ktf_ref, vf_ref, pf_ref, hf_ref), (qb_ref, ktb_ref, vb_ref, pb_ref, hb_ref))

    def body(ci, carry):
        probs = []
        for sub in range(CHUNKS_PER_TRIP):
            for d, (q_ref, kt_ref, v_ref, p_ref, h_ref) in enumerate(dirs):
                step = ci * CHUNKS_PER_TRIP + sub
                c = step if d == 0 else cb - 1 - step
                rows = pl.ds(pl.multiple_of(c * L, L), L)
                a2, mprev2, mm2, eneg, w, decay = [p_ref[0, qi, c] for qi in range(N_PREP)]
                mm2_cols = mm2.T
                eneg_cols = eneg.T
                for hd in range(HEADS):
                    rix = d * HEADS + hd
                    qc = q_ref[0, rows, hd * DK:(hd + 1) * DK]
                    ktc = kt_ref[0, hd * DK:(hd + 1) * DK, rows]
                    vc = v_ref[0, rows, hd * DV:(hd + 1) * DV]
                    probs.append(dict(
                        d=d, hd=hd, rows=rows, h_ref=h_ref, qc=qc, ktc=ktc, s=_dot(qc, ktc),
                        top=jnp.concatenate([vc, ones_blk], axis=1),
                        rowvec=jnp.concatenate([a2[rix:rix + 1], mprev2[rix:rix + 1]], axis=1),
                        mm2=mm2_cols[:, rix:rix + 1], eneg=eneg_cols[:, rix:rix + 1],
                        w=w[rix:rix + 1], dec=decay[rix:rix + 1]))
        state = {(d, hd): st_ref[d, hd] for d in range(2) for hd in range(HEADS)}
        for pr in probs:
            st = state[pr["d"], pr["hd"]]
            dmat = jnp.where(masks[pr["d"]], jnp.exp2(pr["rowvec"] - pr["mm2"]), 0.0)
            lhs = (jnp.concatenate([pr["s"], pr["qc"].astype(F32)], axis=1) * dmat).astype(BF)
            pr["main"] = _dot(lhs, jnp.concatenate([pr["top"], st.astype(BF)], axis=0))
            ktw = (pr["ktc"].astype(F32) * pr["w"]).astype(BF)
            dec = pr["dec"]
            state[pr["d"], pr["hd"]] = jnp.concatenate([dec, dec, dec], axis=1) * st + _dot(ktw, pr["top"])
        for pr in probs:
            main = pr["main"]
            r = 1.0 / jnp.maximum(jnp.abs(main[:, DV:]), pr["eneg"])
            h = main[:, 0:DV] * jnp.concatenate([r, r], axis=1)
            pr["h_ref"][0, pr["rows"], pr["hd"] * DV:(pr["hd"] + 1) * DV] = h.astype(BF)
        for (d, hd), st in state.items():
            st_ref[d, hd] = st
        return carry

    lax.fori_loop(0, cb // CHUNKS_PER_TRIP, body, 0)


def _scan(q, kt, v, prep, sb):
    bsz, s, _ = q.shape
    nb = s // sb
    cb = sb // CHUNK
    fwd3 = lambda b, j: (b, j, 0)
    bwd3 = lambda b, j: (b, nb - 1 - j, 0)
    in_specs = []
    for blk3, blkt, blkp in ((fwd3, lambda b, j: (b, 0, j), lambda b, j: (b, 0, j, 0, 0)),
                             (bwd3, lambda b, j: (b, 0, nb - 1 - j), lambda b, j: (b, 0, nb - 1 - j, 0, 0))):
        in_specs += [pl.BlockSpec((1, sb, QK_WIDTH), blk3), pl.BlockSpec((1, QK_WIDTH, sb), blkt),
                     pl.BlockSpec((1, sb, V_WIDTH), blk3), pl.BlockSpec((1, N_PREP, cb, N_PROBLEMS, CHUNK), blkp)]
    return pl.pallas_call(
        _scan_body,
        grid=(bsz, nb),
        in_specs=in_specs,
        out_specs=[pl.BlockSpec((1, sb, V_WIDTH), fwd3), pl.BlockSpec((1, sb, V_WIDTH), bwd3)],
        out_shape=[jax.ShapeDtypeStruct((bsz, s, V_WIDTH), BF)] * 2,
        scratch_shapes=[pltpu.VMEM((2, HEADS, DK, DV + CHUNK), F32)],
        compiler_params=_params(("parallel", "arbitrary"), V7X_SCOPED_VMEM_BYTES),
        name="mlstm_scan",
    )(q, kt, v, prep, q, kt, v, prep)


def _mlp(xn, w1_ref, w2_ref, ff_chunk):
    d_ff = w1_ref.shape[2]
    acc = None
    for c0 in range(0, d_ff, ff_chunk):
        a = _dot(xn, w1_ref[0, :, c0:c0 + ff_chunk])
        a = jnp.square(jnp.maximum(a, 0.0)).astype(BF)
        part = _dot(a, w2_ref[0, c0:c0 + ff_chunk, :])
        acc = part if acc is None else acc + part
    return acc


def _mlp_and_embed(h, p_ref, nmlp_ref, w1_ref, w2_ref, nple_ref, gw_ref, gb_ref, plew_ref, ff_chunk):
    h = h + _mlp(_rmsnorm(h, nmlp_ref[...]).astype(BF), w1_ref, w2_ref, ff_chunk)
    gate = jax.nn.sigmoid(_dot(_rmsnorm(h, nple_ref[...]).astype(BF), gw_ref[0]) + gb_ref[...])
    return h + gate * _dot(p_ref[...].astype(BF), plew_ref[0])


def _tail0_body(hf_ref, hb_ref, so_ref, x_ref, p_ref, hnorm_ref, wout_ref, nmlp_ref, w1_ref, w2_ref, nple_ref,
                gw_ref, gb_ref, plew_ref, nmix_ref, pin_ref, h_ref, u_ref, *, ff_chunk):
    hs = hf_ref[...].astype(F32) + hb_ref[...].astype(F32)
    parts = []
    for hd in range(HEADS):
        blk = hs[:, hd * DV:(hd + 1) * DV]
        ms = jnp.mean(blk * blk, axis=-1, keepdims=True)
        parts.append(blk * lax.rsqrt(ms + EPS))
    hn = jnp.concatenate(parts, axis=1) * hnorm_ref[...] * so_ref[...].astype(F32)
    h = x_ref[...] + _dot(hn.astype(BF), wout_ref[...])
    h = _mlp_and_embed(h, p_ref, nmlp_ref, w1_ref, w2_ref, nple_ref, gw_ref, gb_ref, plew_ref, ff_chunk)
    h_ref[...] = h
    u_ref[...] = _dot(_rmsnorm(h, nmix_ref[...]).astype(BF), pin_ref[...])


def _tail0(hf, hb, so, x2, p2, weights, tm, ff_chunk):
    m, d = x2.shape
    return pl.pallas_call(
        functools.partial(_tail0_body, ff_chunk=ff_chunk),
        grid=(m // tm,),
        in_specs=[_rows(tm, d), _rows(tm, d), _rows(tm, d), _rows(tm, d), _rows(tm, p2.shape[1])]
                 + [spec for _, spec in weights],
        out_specs=[_rows(tm, d), _rows(tm, d)],
        out_shape=[jax.ShapeDtypeStruct((m, d), F32), jax.ShapeDtypeStruct((m, d), F32)],
        compiler_params=_params(("parallel",), V7X_SCOPED_VMEM_BYTES),
        name="layer0_tail",
    )(hf, hb, so, x2, p2, *[w for w, _ in weights])


def _layer1_body(u_ref, uprev_ref, unext_ref, h_ref, p_ref, wgrp_ref, scale_ref, pout_ref, nmlp_ref, w1_ref,
                 w2_ref, nple_ref, gw_ref, gb_ref, plew_ref, nfin_ref, out_ref, ext_ref, lvl_ref, *, seq, ff_chunk):
    tm, d = u_ref.shape
    gw = d // len(POOL_WINDOWS)
    rows = tm + 2 * POOL_HALO
    t0 = (pl.program_id(0) * tm) % seq
    ext_ref[0:POOL_HALO, :] = jnp.where(t0 > 0, uprev_ref[...], 0.0)
    ext_ref[POOL_HALO:POOL_HALO + tm, :] = u_ref[...]
    ext_ref[POOL_HALO + tm:rows, :] = jnp.where(t0 + tm < seq, unext_ref[...], 0.0)
    ext_ref[rows:, :] = jnp.zeros((POOL_HALO, d), F32)
    lvl_ref[:, rows:, :] = jnp.zeros((2, POOL_HALO, gw), F32)

    def window_count(first_row, win):
        t = first_row + lax.broadcasted_iota(jnp.int32, (POOL_HALO, gw), 0)
        return (jnp.minimum(t + (win - win // 2), seq) - jnp.maximum(t - win // 2, 0)).astype(F32)

    mixed = []
    for gi, win in enumerate(POOL_WINDOWS):
        cols = slice(gi * gw, (gi + 1) * gw)
        load = lambda off, n, cols=cols: ext_ref[pl.ds(off, n), cols]
        span, slot = 1, 0
        while 2 * span < win:
            lvl_ref[slot, 0:rows, :] = load(0, rows) + load(span, rows)
            load = lambda off, n, slot=slot: lvl_ref[slot, pl.ds(off, n), :]
            span, slot = 2 * span, 1 - slot
        start = POOL_HALO - win // 2
        total = load(start, tm) + load(start + span, tm)
        mean = jnp.concatenate([total[0:POOL_HALO] / window_count(t0, win),
                                total[POOL_HALO:tm - POOL_HALO] * (1.0 / win),
                                total[tm - POOL_HALO:] / window_count(t0 + tm - POOL_HALO, win)], axis=0)
        mixed.append(_dot((mean - u_ref[:, cols]).astype(BF), wgrp_ref[gi]))
    y = jnp.concatenate(mixed, axis=1) * scale_ref[...]
    h = h_ref[...] + _dot(y.astype(BF), pout_ref[...])
    h = _mlp_and_embed(h, p_ref, nmlp_ref, w1_ref, w2_ref, nple_ref, gw_ref, gb_ref, plew_ref, ff_chunk)
    out_ref[...] = _rmsnorm(h, nfin_ref[...])


def _layer1(u, h, p2, weights, tm, seq, ff_chunk):
    m, d = u.shape
    n_tiles = m // tm
    hpt = tm // POOL_HALO
    last_halo_block = m // POOL_HALO - 1
    gw = d // len(POOL_WINDOWS)
    return pl.pallas_call(
        functools.partial(_layer1_body, seq=seq, ff_chunk=ff_chunk),
        grid=(n_tiles,),
        in_specs=[_rows(tm, d),
                  pl.BlockSpec((POOL_HALO, d), lambda i: (jnp.maximum(i * hpt - 1, 0), 0)),
                  pl.BlockSpec((POOL_HALO, d), lambda i: (jnp.minimum((i + 1) * hpt, last_halo_block), 0)),
                  _rows(tm, d), _rows(tm, p2.shape[1], first_tile=n_tiles)]
                 + [spec for _, spec in weights],
        out_specs=_rows(tm, d),
        out_shape=jax.ShapeDtypeStruct((m, d), F32),
        scratch_shapes=[pltpu.VMEM((tm + 3 * POOL_HALO, d), F32), pltpu.VMEM((2, tm + 3 * POOL_HALO, gw), F32)],
        compiler_params=_params(("parallel",), V7X_SCOPED_VMEM_BYTES),
        name="layer1_pool_mlp",
    )(u, u, u, h, p2, *[w for w, _ in weights])


def _tile_plan(seq):
    tm_proj, tm_fused, ff_chunk, scan_block = min(1024, seq), min(512, seq), 1024, min(2048, seq)
    assert seq % scan_block == 0 and seq % tm_fused == 0 and seq % tm_proj == 0
    assert scan_block % (CHUNK * CHUNKS_PER_TRIP) == 0 and tm_proj % CHUNK == 0 and tm_fused >= 2 * POOL_HALO
    return tm_proj, tm_fused, ff_chunk, scan_block


def kernel(x, p, norm_mix, norm_mlp, norm_ple, norm_final, mlstm_w_in, mlstm_b_gates, mlstm_head_norm,
           mlstm_w_out, pool_w_in, pool_w_grp, pool_scale, pool_w_out, mlp_w1, mlp_w2, ple_w, ple_gate_w,
           ple_gate_b):
    bsz, seq, d = x.shape
    m = bsz * seq
    tm_proj, tm_fused, ff_chunk, scan_block = _tile_plan(seq)
    row = lambda v: v.reshape(1, -1)
    bf = lambda w: w.astype(BF)

    def whole(v):
        return v, _resident(v.shape)

    def layer(stacked, i):
        return stacked, _layer_of(stacked, i)

    x2 = x.reshape(m, d)
    p2 = p.reshape(-1, p.shape[-1])
    n_main = 2 * QK_WIDTH + 2 * V_WIDTH
    w_in = bf(mlstm_w_in)
    w_kt = w_in[0, :, QK_WIDTH:2 * QK_WIDTH].T
    gate_perm = jnp.array([0, 1, 2, 3, 8, 9, 10, 11, 4, 5, 6, 7, 12, 13, 14, 15], jnp.int32)
    w_gt = w_in[0, :, n_main:].T[gate_perm]
    bias_rows = jnp.broadcast_to(mlstm_b_gates[0].reshape(N_GATES)[gate_perm][:, None], (N_GATES, CHUNK))

    later = [mlp_w1, mlp_w2, ple_gate_w, ple_w, mlstm_w_out, pool_w_in, pool_w_grp, pool_w_out]
    flat = [w.reshape(-1, w.shape[-1]) for w in later]
    steps = m // tm_proj
    assert all(w.shape[0] % (BF16_TILE_ROWS * steps) == 0 for w in flat)
    (q, kt, v, so, gt), cast = _inproj(x2, row(norm_mix[0]), w_in, w_kt, w_gt, flat, tm_proj, seq)
    w1, w2, gate_w, emb_w, w_out, pool_in, pool_grp, pool_out = [c.reshape(w.shape) for c, w in zip(cast, later)]
    prep = _prep(gt, bias_rows)
    hf, hb = _scan(q.reshape(bsz, seq, QK_WIDTH), kt, v.reshape(bsz, seq, V_WIDTH), prep, scan_block)

    def mlp_embed_weights(i):
        return [whole(row(norm_mlp[i])), layer(w1, i), layer(w2, i), whole(row(norm_ple[i])), layer(gate_w, i),
                whole(row(ple_gate_b[i])), layer(emb_w, i)]

    tail_weights = ([whole(row(mlstm_head_norm[0])), whole(w_out[0])] + mlp_embed_weights(0)
                    + [whole(row(norm_mix[1])), whole(pool_in[0])])
    h, u = _tail0(hf.reshape(m, V_WIDTH), hb.reshape(m, V_WIDTH), so, x2, p2, tail_weights, tm_fused, ff_chunk)

    l1_weights = ([whole(pool_grp[0]), whole(row(pool_scale[0])), whole(pool_out[0])]
                  + mlp_embed_weights(1) + [whole(row(norm_final))])
    out = _layer1(u, h, p2, l1_weights, tm_fused, seq, ff_chunk)
    return out.reshape(bsz, seq, d)
```

```python
import functools

import jax
import jax.numpy as jnp
from jax import lax
from jax.experimental import pallas as pl
from jax.experimental.pallas import tpu as pltpu

EPS = 1e-6
HEADS = 4
DK = 128
DV = 256
CHUNK = 128
QK_WIDTH = HEADS * DK
V_WIDTH = HEADS * DV
N_GATES = 4 * HEADS
POOL_WINDOWS = (2, 4, 8, 16)
POOL_HALO = 8
N_PROBLEMS = 2 * HEADS
N_PREP = 6
CHUNKS_PER_TRIP = 2

LANES = 128
BF16_TILE_ROWS = 16
V7X_SCOPED_VMEM_BYTES = 60000 * 1024

LOG2E = 1.4426950408889634

BF = jnp.bfloat16
F32 = jnp.float32


def _dot(a, b):
    return jnp.dot(a, b, preferred_element_type=F32)


def _rmsnorm(x, gain):
    ms = jnp.mean(x * x, axis=-1, keepdims=True)
    return x * lax.rsqrt(ms + EPS) * gain


def _resident(shape):
    nd = len(shape)
    return pl.BlockSpec(shape, lambda *_: (0,) * nd, pipeline_mode=pl.Buffered(1))


def _layer_of(stacked, layer):
    nd = stacked.ndim
    return pl.BlockSpec((1,) + stacked.shape[1:], lambda *_: (layer,) + (0,) * (nd - 1),
                        pipeline_mode=pl.Buffered(1))


def _rows(tm, width, first_tile=0):
    return pl.BlockSpec((tm, width), lambda i: (i + first_tile, 0))


def _params(sem, vmem=None):
    return pltpu.CompilerParams(dimension_semantics=sem, vmem_limit_bytes=vmem)


def _inproj_body(x_ref, gain_ref, w32_ref, wkt_ref, wgt_ref, *refs, n_cast):
    cast_src, (q_ref, kt_ref, v_ref, so_ref, gt_ref) = refs[:n_cast], refs[n_cast:n_cast + 5]
    cast_dst, w_ref = refs[n_cast + 5:2 * n_cast + 5], refs[2 * n_cast + 5]

    @pl.when(pl.program_id(0) == 0)
    def _():
        w_ref[...] = w32_ref[...].astype(BF)

    for src, dst in zip(cast_src, cast_dst):
        dst[...] = src[...].astype(BF)
    v0 = 2 * QK_WIDTH
    nt = (((1,), (1,)), ((), ()))
    xn = _rmsnorm(x_ref[...], gain_ref[...]).astype(BF)
    q_ref[...] = _dot(xn, w_ref[0, :, 0:QK_WIDTH]).astype(BF)
    kt = lax.dot_general(wkt_ref[...], xn, nt, preferred_element_type=F32)
    kt_ref[0] = (kt * (DK ** -0.5)).astype(BF)
    v_ref[...] = _dot(xn, w_ref[0, :, v0:v0 + V_WIDTH]).astype(BF)
    so_ref[...] = jax.nn.sigmoid(_dot(xn, w_ref[0, :, v0 + V_WIDTH:v0 + 2 * V_WIDTH])).astype(BF)
    gt = lax.dot_general(wgt_ref[...], xn, nt, preferred_element_type=F32)
    for c in range(gt_ref.shape[1]):
        gt_ref[0, c] = gt[:, c * CHUNK:(c + 1) * CHUNK]


def _inproj(x2, gain, w_in, w_kt, w_gt, to_cast, tm, seq):
    m, d = x2.shape
    steps = m // tm
    tiles_per_seq = seq // tm
    cpt = tm // CHUNK
    per_seq = lambda i: (i // tiles_per_seq, i % tiles_per_seq)
    slabs = [_rows(w.shape[0] // steps, w.shape[1]) for w in to_cast]
    out = pl.pallas_call(
        functools.partial(_inproj_body, n_cast=len(to_cast)),
        grid=(steps,),
        in_specs=[_rows(tm, d), _resident(gain.shape), _layer_of(w_in, 0), _resident(w_kt.shape),
                  _resident(w_gt.shape)] + slabs,
        out_specs=[_rows(tm, QK_WIDTH),
                   pl.BlockSpec((1, QK_WIDTH, tm), lambda i: (per_seq(i)[0], 0, per_seq(i)[1])),
                   _rows(tm, V_WIDTH), _rows(tm, V_WIDTH),
                   pl.BlockSpec((1, cpt, N_GATES, CHUNK), lambda i: (per_seq(i)[0], per_seq(i)[1], 0, 0))] + slabs,
        out_shape=[jax.ShapeDtypeStruct((m, QK_WIDTH), BF), jax.ShapeDtypeStruct((m // seq, QK_WIDTH, seq), BF),
                   jax.ShapeDtypeStruct((m, V_WIDTH), BF), jax.ShapeDtypeStruct((m, V_WIDTH), BF),
                   jax.ShapeDtypeStruct((m // seq, seq // CHUNK, N_GATES, CHUNK), F32)]
                  + [jax.ShapeDtypeStruct(w.shape, BF) for w in to_cast],
        scratch_shapes=[pltpu.VMEM((1,) + w_in.shape[1:], BF)],
        compiler_params=_params(("arbitrary",), V7X_SCOPED_VMEM_BYTES),
        name="mlstm_inproj",
    )(x2, gain, w_in, w_kt, w_gt, *to_cast)
    return out[:5], out[5:]


def _log_sigmoid(x):
    return jnp.minimum(x, 0.0) - jnp.log1p(jnp.exp(-jnp.abs(x)))


def _split3(x):
    hi = x.astype(BF)
    r1 = x - hi.astype(F32)
    mid = r1.astype(BF)
    lo = (r1 - mid.astype(F32)).astype(BF)
    return hi, mid, lo


def _prep_body(gt_ref, bias_ref, out_ref, mprev_ref):
    nc = gt_ref.shape[1]
    L = CHUNK
    g = gt_ref[0] + bias_ref[...][None]
    i_pre = g[:, 0:N_PROBLEMS, :].reshape(nc * N_PROBLEMS, L)
    logf = _log_sigmoid(g[:, N_PROBLEMS:2 * N_PROBLEMS, :]).reshape(nc * N_PROBLEMS, L)

    row = lax.broadcasted_iota(jnp.int32, (nc * N_PROBLEMS, L), 0)
    lane = lax.broadcasted_iota(jnp.int32, (nc * N_PROBLEMS, L), 1)
    is_fwd = (row % N_PROBLEMS) < HEADS

    s_idx = lax.broadcasted_iota(jnp.int32, (L, L), 0)
    t_idx = lax.broadcasted_iota(jnp.int32, (L, L), 1)
    tri_pre = (s_idx <= t_idx).astype(BF)
    tri_suf = (s_idx >= t_idx).astype(BF)
    pre = jnp.zeros((nc * N_PROBLEMS, L), F32)
    suf = jnp.zeros((nc * N_PROBLEMS, L), F32)
    for piece in _split3(logf):
        pre = pre + _dot(piece, tri_pre)
        suf = suf + _dot(piece, tri_suf)
    b = jnp.where(is_fwd, pre, suf)
    a = i_pre - b

    pm = a
    sm = a
    k = 1
    while k < L:
        pm = jnp.where(lane >= k, jnp.maximum(pm, pltpu.roll(pm, k, 1)), pm)
        sm = jnp.where(lane < L - k, jnp.maximum(sm, pltpu.roll(sm, L - k, 1)), sm)
        k *= 2
    cm = jnp.where(is_fwd, pm, sm)

    amax = jnp.broadcast_to(jnp.max(a, axis=1, keepdims=True), (nc * N_PROBLEMS, L)).reshape(nc, N_PROBLEMS, L)
    b_last = jnp.broadcast_to(jnp.sum(logf, axis=1, keepdims=True), (nc * N_PROBLEMS, L)).reshape(nc, N_PROBLEMS, L)

    fwd_rows = lax.broadcasted_iota(jnp.int32, (N_PROBLEMS, L), 0) < HEADS
    m = jnp.zeros((N_PROBLEMS, L), F32)
    for c in range(nc):
        cb = nc - 1 - c
        mprev_ref[c, 0:HEADS, :] = m[0:HEADS]
        mprev_ref[cb, HEADS:, :] = m[HEADS:]
        am = jnp.where(fwd_rows, amax[c], amax[cb])
        bl = jnp.where(fwd_rows, b_last[c], b_last[cb])
        m = bl + jnp.maximum(m, am)

    mprev = mprev_ref[...]
    a3 = a.reshape(nc, N_PROBLEMS, L)
    b3 = b.reshape(nc, N_PROBLEMS, L)
    cm3 = cm.reshape(nc, N_PROBLEMS, L)
    sigma = jnp.maximum(mprev, amax)
    mm = jnp.maximum(mprev, cm3)
    out_ref[0, 0] = a3 * LOG2E
    out_ref[0, 1] = mprev * LOG2E
    out_ref[0, 2] = mm * LOG2E
    out_ref[0, 3] = jnp.exp(-(b3 + mm))
    out_ref[0, 4] = jnp.exp(a3 - sigma)
    out_ref[0, 5] = jnp.exp(mprev - sigma)


def _prep(gt, bias_rows):
    bsz, nc, _, L = gt.shape
    return pl.pallas_call(
        _prep_body,
        grid=(bsz,),
        in_specs=[pl.BlockSpec((1, nc, N_GATES, L), lambda b: (b, 0, 0, 0)), _resident(bias_rows.shape)],
        out_specs=pl.BlockSpec((1, N_PREP, nc, N_PROBLEMS, L), lambda b: (b, 0, 0, 0, 0)),
        out_shape=jax.ShapeDtypeStruct((bsz, N_PREP, nc, N_PROBLEMS, L), F32),
        scratch_shapes=[pltpu.VMEM((nc, N_PROBLEMS, L), F32)],
        compiler_params=_params(("parallel",)),
        name="mlstm_gate_prep",
    )(gt, bias_rows)


def _scan_body(qf_ref, ktf_ref, vf_ref, pf_ref, qb_ref, ktb_ref, vb_ref, pb_ref, hf_ref, hb_ref, st_ref):
    L = CHUNK
    cb = pf_ref.shape[2]

    @pl.when(pl.program_id(1) == 0)
    def _():
        st_ref[...] = jnp.zeros(st_ref.shape, F32)

    t_idx = lax.broadcasted_iota(jnp.int32, (L, 2 * L), 0)
    s_idx = lax.broadcasted_iota(jnp.int32, (L, 2 * L), 1)
    masks = ((s_idx <= t_idx) | (s_idx >= L), (s_idx >= t_idx))
    ones_blk = jnp.ones((L, L), BF)

    dirs = ((qf_ref, ktf_ref, vf_ref, pf_ref, hf_ref), (qb_ref, ktb_ref, vb_ref, pb_ref, hb_ref))

    def body(ci, carry):
        probs = []
        for sub in range(CHUNKS_PER_TRIP):
            for d, (q_ref, kt_ref, v_ref, p_ref, h_ref) in enumerate(dirs):
                step = ci * CHUNKS_PER_TRIP + sub
                c = step if d == 0 else cb - 1 - step
                rows = pl.ds(pl.multiple_of(c * L, L), L)
                a2, mprev2, mm2, eneg, w, decay = [p_ref[0, qi, c] for qi in range(N_PREP)]
                mm2_cols = mm2.T
                eneg_cols = eneg.T
                for hd in range(HEADS):
                    rix = d * HEADS + hd
                    qc = q_ref[0, rows, hd * DK:(hd + 1) * DK]
                    ktc = kt_ref[0, hd * DK:(hd + 1) * DK, rows]
                    vc = v_ref[0, rows, hd * DV:(hd + 1) * DV]
                    probs.append(dict(
                        d=d, hd=hd, rows=rows, h_ref=h_ref, qc=qc, ktc=ktc, s=_dot(qc, ktc),
                        top=jnp.concatenate([vc, ones_blk], axis=1),
                        rowvec=jnp.concatenate([a2[rix:rix + 1], mprev2[rix:rix + 1]], axis=1),
                        mm2=mm2_cols[:, rix:rix + 1], eneg=eneg_cols[:, rix:rix + 1],
                        w=w[rix:rix + 1], dec=decay[rix:rix + 1]))
        state = {(d, hd): st_ref[d, hd] for d in range(2) for hd in range(HEADS)}
        for pr in probs:
            st = state[pr["d"], pr["hd"]]
            dmat = jnp.where(masks[pr["d"]], jnp.exp2(pr["rowvec"] - pr["mm2"]), 0.0)
            lhs = (jnp.concatenate([pr["s"], pr["qc"].astype(F32)], axis=1) * dmat).astype(BF)
            pr["main"] = _dot(lhs, jnp.concatenate([pr["top"], st.astype(BF)], axis=0))
            ktw = (pr["ktc"].astype(F32) * pr["w"]).astype(BF)
            dec = pr["dec"]
            state[pr["d"], pr["hd"]] = jnp.concatenate([dec, dec, dec], axis=1) * st + _dot(ktw, pr["top"])
        for pr in probs:
            main = pr["main"]
            r = 1.0 / jnp.maximum(jnp.abs(main[:, DV:]), pr["eneg"])
            h = main[:, 0:DV] * jnp.concatenate([r, r], axis=1)
            pr["h_ref"][0, pr["rows"], pr["hd"] * DV:(pr["hd"] + 1) * DV] = h.astype(BF)
        for (d, hd), st in state.items():
            st_ref[d, hd] = st
        return carry

    lax.fori_loop(0, cb // CHUNKS_PER_TRIP, body, 0)


def _scan(q, kt, v, prep, sb):
    bsz, s, _ = q.shape
    nb = s // sb
    cb = sb // CHUNK
    fwd3 = lambda b, j: (b, j, 0)
    bwd3 = lambda b, j: (b, nb - 1 - j, 0)
    in_specs = []
    for blk3, blkt, blkp in ((fwd3, lambda b, j: (b, 0, j), lambda b, j: (b, 0, j, 0, 0)),
                             (bwd3, lambda b, j: (b, 0, nb - 1 - j), lambda b, j: (b, 0, nb - 1 - j, 0, 0))):
        in_specs += [pl.BlockSpec((1, sb, QK_WIDTH), blk3), pl.BlockSpec((1, QK_WIDTH, sb), blkt),
                     pl.BlockSpec((1, sb, V_WIDTH), blk3), pl.BlockSpec((1, N_PREP, cb, N_PROBLEMS, CHUNK), blkp)]
    return pl.pallas_call(
        _scan_body,
        grid=(bsz, nb),
        in_specs=in_specs,
        out_specs=[pl.BlockSpec((1, sb, V_WIDTH), fwd3), pl.BlockSpec((1, sb, V_WIDTH), bwd3)],
        out_shape=[jax.ShapeDtypeStruct((bsz, s, V_WIDTH), BF)] * 2,
        scratch_shapes=[pltpu.VMEM((2, HEADS, DK, DV + CHUNK), F32)],
        compiler_params=_params(("parallel", "arbitrary"), V7X_SCOPED_VMEM_BYTES),
        name="mlstm_scan",
    )(q, kt, v, prep, q, kt, v, prep)


def _mlp(xn, w1_ref, w2_ref, ff_chunk):
    d_ff = w1_ref.shape[2]
    acc = None
    for c0 in range(0, d_ff, ff_chunk):
        a = _dot(xn, w1_ref[0, :, c0:c0 + ff_chunk])
        a = jnp.square(jnp.maximum(a, 0.0)).astype(BF)
        part = _dot(a, w2_ref[0, c0:c0 + ff_chunk, :])
        acc = part if acc is None else acc + part
    return acc


def _mlp_and_embed(h, p_ref, nmlp_ref, w1_ref, w2_ref, nple_ref, gw_ref, gb_ref, plew_ref, ff_chunk):
    h = h + _mlp(_rmsnorm(h, nmlp_ref[...]).astype(BF), w1_ref, w2_ref, ff_chunk)
    gate = jax.nn.sigmoid(_dot(_rmsnorm(h, nple_ref[...]).astype(BF), gw_ref[0]) + gb_ref[...])
    return h + gate * _dot(p_ref[...].astype(BF), plew_ref[0])


def _tail0_body(hf_ref, hb_ref, so_ref, x_ref, p_ref, hnorm_ref, wout_ref, nmlp_ref, w1_ref, w2_ref, nple_ref,
                gw_ref, gb_ref, plew_ref, nmix_ref, pin_ref, h_ref, u_ref, *, ff_chunk):
    hs = hf_ref[...].astype(F32) + hb_ref[...].astype(F32)
    parts = []
    for hd in range(HEADS):
        blk = hs[:, hd * DV:(hd + 1) * DV]
        ms = jnp.mean(blk * blk, axis=-1, keepdims=True)
        parts.append(blk * lax.rsqrt(ms + EPS))
    hn = jnp.concatenate(parts, axis=1) * hnorm_ref[...] * so_ref[...].astype(F32)
    h = x_ref[...] + _dot(hn.astype(BF), wout_ref[...])
    h = _mlp_and_embed(h, p_ref, nmlp_ref, w1_ref, w2_ref, nple_ref, gw_ref, gb_ref, plew_ref, ff_chunk)
    h_ref[...] = h
    u_ref[...] = _dot(_rmsnorm(h, nmix_ref[...]).astype(BF), pin_ref[...])


def _tail0(hf, hb, so, x2, p2, weights, tm, ff_chunk):
    m, d = x2.shape
    return pl.pallas_call(
        functools.partial(_tail0_body, ff_chunk=ff_chunk),
        grid=(m // tm,),
        in_specs=[_rows(tm, d), _rows(tm, d), _rows(tm, d), _rows(tm, d), _rows(tm, p2.shape[1])]
                 + [spec for _, spec in weights],
        out_specs=[_rows(tm, d), _rows(tm, d)],
        out_shape=[jax.ShapeDtypeStruct((m, d), F32), jax.ShapeDtypeStruct((m, d), F32)],
        compiler_params=_params(("parallel",), V7X_SCOPED_VMEM_BYTES),
        name="layer0_tail",
    )(hf, hb, so, x2, p2, *[w for w, _ in weights])


def _layer1_body(u_ref, uprev_ref, unext_ref, h_ref, p_ref, wgrp_ref, scale_ref, pout_ref, nmlp_ref, w1_ref,
                 w2_ref, nple_ref, gw_ref, gb_ref, plew_ref, nfin_ref, out_ref, ext_ref, lvl_ref, *, seq, ff_chunk):
    tm, d = u_ref.shape
    gw = d // len(POOL_WINDOWS)
    rows = tm + 2 * POOL_HALO
    t0 = (pl.program_id(0) * tm) % seq
    ext_ref[0:POOL_HALO, :] = jnp.where(t0 > 0, uprev_ref[...], 0.0)
    ext_ref[POOL_HALO:POOL_HALO + tm, :] = u_ref[...]
    ext_ref[POOL_HALO + tm:rows, :] = jnp.where(t0 + tm < seq, unext_ref[...], 0.0)
    ext_ref[rows:, :] = jnp.zeros((POOL_HALO, d), F32)
    lvl_ref[:, rows:, :] = jnp.zeros((2, POOL_HALO, gw), F32)

    def window_count(first_row, win):
        t = first_row + lax.broadcasted_iota(jnp.int32, (POOL_HALO, gw), 0)
        return (jnp.minimum(t + (win - win // 2), seq) - jnp.maximum(t - win // 2, 0)).astype(F32)

    mixed = []
    for gi, win in enumerate(POOL_WINDOWS):
        cols = slice(gi * gw, (gi + 1) * gw)
        load = lambda off, n, cols=cols: ext_ref[pl.ds(off, n), cols]
        span, slot = 1, 0
        while 2 * span < win:
            lvl_ref[slot, 0:rows, :] = load(0, rows) + load(span, rows)
            load = lambda off, n, slot=slot: lvl_ref[slot, pl.ds(off, n), :]
            span, slot = 2 * span, 1 - slot
        start = POOL_HALO - win // 2
        total = load(start, tm) + load(start + span, tm)
        mean = jnp.concatenate([total[0:POOL_HALO] / window_count(t0, win),
                                total[POOL_HALO:tm - POOL_HALO] * (1.0 / win),
                                total[tm - POOL_HALO:] / window_count(t0 + tm - POOL_HALO, win)], axis=0)
        mixed.append(_dot((mean - u_ref[:, cols]).astype(BF), wgrp_ref[gi]))
    y = jnp.concatenate(mixed, axis=1) * scale_ref[...]
    h = h_ref[...] + _dot(y.astype(BF), pout_ref[...])
    h = _mlp_and_embed(h, p_ref, nmlp_ref, w1_ref, w2_ref, nple_ref, gw_ref, gb_ref, plew_ref, ff_chunk)
    out_ref[...] = _rmsnorm(h, nfin_ref[...])


def _layer1(u, h, p2, weights, tm, seq, ff_chunk):
    m, d = u.shape
    n_tiles = m // tm
    hpt = tm // POOL_HALO
    last_halo_block = m // POOL_HALO - 1
    gw = d // len(POOL_WINDOWS)
    return pl.pallas_call(
        functools.partial(_layer1_body, seq=seq, ff_chunk=ff_chunk),
        grid=(n_tiles,),
        in_specs=[_rows(tm, d),
                  pl.BlockSpec((POOL_HALO, d), lambda i: (jnp.maximum(i * hpt - 1, 0), 0)),
                  pl.BlockSpec((POOL_HALO, d), lambda i: (jnp.minimum((i + 1) * hpt, last_halo_block), 0)),
                  _rows(tm, d), _rows(tm, p2.shape[1], first_tile=n_tiles)]
                 + [spec for _, spec in weights],
        out_specs=_rows(tm, d),
        out_shape=jax.ShapeDtypeStruct((m, d), F32),
        scratch_shapes=[pltpu.VMEM((tm + 3 * POOL_HALO, d), F32), pltpu.VMEM((2, tm + 3 * POOL_HALO, gw), F32)],
        compiler_params=_params(("parallel",), V7X_SCOPED_VMEM_BYTES),
        name="layer1_pool_mlp",
    )(u, u, u, h, p2, *[w for w, _ in weights])


def _tile_plan(seq):
    tm_proj, tm_fused, ff_chunk, scan_block = min(1024, seq), min(512, seq), 1024, min(2048, seq)
    assert seq % scan_block == 0 and seq % tm_fused == 0 and seq % tm_proj == 0
    assert scan_block % (CHUNK * CHUNKS_PER_TRIP) == 0 and tm_proj % CHUNK == 0 and tm_fused >= 2 * POOL_HALO
    return tm_proj, tm_fused, ff_chunk, scan_block


def kernel(x, p, norm_mix, norm_mlp, norm_ple, norm_final, mlstm_w_in, mlstm_b_gates, mlstm_head_norm,
           mlstm_w_out, pool_w_in, pool_w_grp, pool_scale, pool_w_out, mlp_w1, mlp_w2, ple_w, ple_gate_w,
           ple_gate_b):
    bsz, seq, d = x.shape
    m = bsz * seq
    tm_proj, tm_fused, ff_chunk, scan_block = _tile_plan(seq)
    row = lambda v: v.reshape(1, -1)
    bf = lambda w: w.astype(BF)

    def whole(v):
        return v, _resident(v.shape)

    def layer(stacked, i):
        return stacked, _layer_of(stacked, i)

    x2 = x.reshape(m, d)
    p2 = p.reshape(-1, p.shape[-1])
    n_main = 2 * QK_WIDTH + 2 * V_WIDTH
    w_in = mlstm_w_in
    w_kt = bf(w_in[0, :, QK_WIDTH:2 * QK_WIDTH].T)
    gate_perm = jnp.array([0, 1, 2, 3, 8, 9, 10, 11, 4, 5, 6, 7, 12, 13, 14, 15], jnp.int32)
    w_gt = bf(w_in[0, :, n_main:].T[gate_perm])
    bias_rows = jnp.broadcast_to(mlstm_b_gates[0].reshape(N_GATES)[gate_perm][:, None], (N_GATES, CHUNK))

    later = [mlp_w1, mlp_w2, ple_gate_w, ple_w, mlstm_w_out, pool_w_in, pool_w_grp, pool_w_out]
    flat = [w.reshape(-1, w.shape[-1]) for w in later]
    steps = m // tm_proj
    assert all(w.shape[0] % (BF16_TILE_ROWS * steps) == 0 for w in flat)
    (q, kt, v, so, gt), cast = _inproj(x2, row(norm_mix[0]), w_in, w_kt, w_gt, flat, tm_proj, seq)
    w1, w2, gate_w, emb_w, w_out, pool_in, pool_grp, pool_out = [c.reshape(w.shape) for c, w in zip(cast, later)]
    prep = _prep(gt, bias_rows)
    hf, hb = _scan(q.reshape(bsz, seq, QK_WIDTH), kt, v.reshape(bsz, seq, V_WIDTH), prep, scan_block)

    def mlp_embed_weights(i):
        return [whole(row(norm_mlp[i])), layer(w1, i), layer(w2, i), whole(row(norm_ple[i])), layer(gate_w, i),
                whole(row(ple_gate_b[i])), layer(emb_w, i)]

    tail_weights = ([whole(row(mlstm_head_norm[0])), whole(w_out[0])] + mlp_embed_weights(0)
                    + [whole(row(norm_mix[1])), whole(pool_in[0])])
    h, u = _tail0(hf.reshape(m, V_WIDTH), hb.reshape(m, V_WIDTH), so, x2, p2, tail_weights, tm_fused, ff_chunk)

    l1_weights = ([whole(pool_grp[0]), whole(row(pool_scale[0])), whole(pool_out[0])]
                  + mlp_embed_weights(1) + [whole(row(norm_final))])
    out = _layer1(u, h, p2, l1_weights, tm_fused, seq, ff_chunk)
    return out.reshape(bsz, seq, d)
```

```python
import functools

import jax
import jax.numpy as jnp
from jax import lax
from jax.experimental import pallas as pl
from jax.experimental.pallas import tpu as pltpu

EPS = 1e-6
HEADS = 4
DK = 128
DV = 256
CHUNK = 128
QK_WIDTH = HEADS * DK
V_WIDTH = HEADS * DV
N_GATES = 4 * HEADS
POOL_WINDOWS = (2, 4, 8, 16)
POOL_HALO = 8
N_PROBLEMS = 2 * HEADS
N_PREP = 6
CHUNKS_PER_TRIP = 2

LANES = 128
BF16_TILE_ROWS = 16
V7X_SCOPED_VMEM_BYTES = 60000 * 1024

LOG2E = 1.4426950408889634

BF = jnp.bfloat16
F32 = jnp.float32


def _dot(a, b):
    return jnp.dot(a, b, preferred_element_type=F32)


def _rmsnorm(x, gain=None):
    ms = jnp.mean(x * x, axis=-1, keepdims=True)
    y = x * lax.rsqrt(ms + EPS)
    return y if gain is None else y * gain


def _resident(shape):
    nd = len(shape)
    return pl.BlockSpec(shape, lambda *_: (0,) * nd, pipeline_mode=pl.Buffered(1))


def _layer_of(stacked, layer):
    nd = stacked.ndim
    return pl.BlockSpec((1,) + stacked.shape[1:], lambda *_: (layer,) + (0,) * (nd - 1),
                        pipeline_mode=pl.Buffered(1))


def _rows(tm, width, first_tile=0):
    return pl.BlockSpec((tm, width), lambda i: (i + first_tile, 0))


def _params(sem, vmem=None):
    return pltpu.CompilerParams(dimension_semantics=sem, vmem_limit_bytes=vmem)


def _inproj_body(x_ref, gain_ref, w32_ref, wkt_ref, wgt_ref, *refs, n_cast, gained):
    n_gain = len(gained)
    cast_src, row_gain = refs[:n_cast], dict(zip(gained, refs[n_cast:n_cast + n_gain]))
    q_ref, kt_ref, v_ref, so_ref, gt_ref = refs[n_cast + n_gain:n_cast + n_gain + 5]
    cast_dst, w_ref = refs[n_cast + n_gain + 5:2 * n_cast + n_gain + 5], refs[2 * n_cast + n_gain + 5]

    @pl.when(pl.program_id(0) == 0)
    def _():
        w_ref[...] = w32_ref[...].astype(BF)

    for k, (src, dst) in enumerate(zip(cast_src, cast_dst)):
        slab = src[...] * row_gain[k][...] if k in row_gain else src[...]
        dst[...] = slab.astype(BF)
    v0 = 2 * QK_WIDTH
    nt = (((1,), (1,)), ((), ()))
    xn = _rmsnorm(x_ref[...], gain_ref[...]).astype(BF)
    q_ref[...] = _dot(xn, w_ref[0, :, 0:QK_WIDTH]).astype(BF)
    kt = lax.dot_general(wkt_ref[...], xn, nt, preferred_element_type=F32)
    kt_ref[0] = (kt * (DK ** -0.5)).astype(BF)
    v_ref[...] = _dot(xn, w_ref[0, :, v0:v0 + V_WIDTH]).astype(BF)
    so_ref[...] = jax.nn.sigmoid(_dot(xn, w_ref[0, :, v0 + V_WIDTH:v0 + 2 * V_WIDTH])).astype(BF)
    gt = lax.dot_general(wgt_ref[...], xn, nt, preferred_element_type=F32)
    for c in range(gt_ref.shape[1]):
        gt_ref[0, c] = gt[:, c * CHUNK:(c + 1) * CHUNK]


def _inproj(x2, gain, w_in, w_kt, w_gt, to_cast, row_gains, tm, seq):
    m, d = x2.shape
    steps = m // tm
    tiles_per_seq = seq // tm
    cpt = tm // CHUNK
    per_seq = lambda i: (i // tiles_per_seq, i % tiles_per_seq)
    slabs = [_rows(w.shape[0] // steps, w.shape[1]) for w in to_cast]
    gained = tuple(k for k, g in enumerate(row_gains) if g is not None)
    gain_cols = [row_gains[k] for k in gained]
    out = pl.pallas_call(
        functools.partial(_inproj_body, n_cast=len(to_cast), gained=gained),
        grid=(steps,),
        in_specs=[_rows(tm, d), _resident(gain.shape), _layer_of(w_in, 0), _resident(w_kt.shape),
                  _resident(w_gt.shape)] + slabs + [_rows(g.shape[0] // steps, 1) for g in gain_cols],
        out_specs=[_rows(tm, QK_WIDTH),
                   pl.BlockSpec((1, QK_WIDTH, tm), lambda i: (per_seq(i)[0], 0, per_seq(i)[1])),
                   _rows(tm, V_WIDTH), _rows(tm, V_WIDTH),
                   pl.BlockSpec((1, cpt, N_GATES, CHUNK), lambda i: (per_seq(i)[0], per_seq(i)[1], 0, 0))] + slabs,
        out_shape=[jax.ShapeDtypeStruct((m, QK_WIDTH), BF), jax.ShapeDtypeStruct((m // seq, QK_WIDTH, seq), BF),
                   jax.ShapeDtypeStruct((m, V_WIDTH), BF), jax.ShapeDtypeStruct((m, V_WIDTH), BF),
                   jax.ShapeDtypeStruct((m // seq, seq // CHUNK, N_GATES, CHUNK), F32)]
                  + [jax.ShapeDtypeStruct(w.shape, BF) for w in to_cast],
        scratch_shapes=[pltpu.VMEM((1,) + w_in.shape[1:], BF)],
        compiler_params=_params(("arbitrary",), V7X_SCOPED_VMEM_BYTES),
        name="mlstm_inproj",
    )(x2, gain, w_in, w_kt, w_gt, *to_cast, *gain_cols)
    return out[:5], out[5:]


def _log_sigmoid(x):
    return jnp.minimum(x, 0.0) - jnp.log1p(jnp.exp(-jnp.abs(x)))


def _split3(x):
    hi = x.astype(BF)
    r1 = x - hi.astype(F32)
    mid = r1.astype(BF)
    lo = (r1 - mid.astype(F32)).astype(BF)
    return hi, mid, lo


def _prep_body(gt_ref, bias_ref, out_ref, mprev_ref):
    nc = gt_ref.shape[1]
    L = CHUNK
    g = gt_ref[0] + bias_ref[...][None]
    i_pre = g[:, 0:N_PROBLEMS, :].reshape(nc * N_PROBLEMS, L)
    logf = _log_sigmoid(g[:, N_PROBLEMS:2 * N_PROBLEMS, :]).reshape(nc * N_PROBLEMS, L)

    row = lax.broadcasted_iota(jnp.int32, (nc * N_PROBLEMS, L), 0)
    lane = lax.broadcasted_iota(jnp.int32, (nc * N_PROBLEMS, L), 1)
    is_fwd = (row % N_PROBLEMS) < HEADS

    s_idx = lax.broadcasted_iota(jnp.int32, (L, L), 0)
    t_idx = lax.broadcasted_iota(jnp.int32, (L, L), 1)
    tri_pre = (s_idx <= t_idx).astype(BF)
    tri_suf = (s_idx >= t_idx).astype(BF)
    pre = jnp.zeros((nc * N_PROBLEMS, L), F32)
    suf = jnp.zeros((nc * N_PROBLEMS, L), F32)
    for piece in _split3(logf):
        pre = pre + _dot(piece, tri_pre)
        suf = suf + _dot(piece, tri_suf)
    b = jnp.where(is_fwd, pre, suf)
    a = i_pre - b

    pm = a
    sm = a
    k = 1
    while k < L:
        pm = jnp.where(lane >= k, jnp.maximum(pm, pltpu.roll(pm, k, 1)), pm)
        sm = jnp.where(lane < L - k, jnp.maximum(sm, pltpu.roll(sm, L - k, 1)), sm)
        k *= 2
    cm = jnp.where(is_fwd, pm, sm)

    amax = jnp.broadcast_to(jnp.max(a, axis=1, keepdims=True), (nc * N_PROBLEMS, L)).reshape(nc, N_PROBLEMS, L)
    b_last = jnp.broadcast_to(jnp.sum(logf, axis=1, keepdims=True), (nc * N_PROBLEMS, L)).reshape(nc, N_PROBLEMS, L)

    fwd_rows = lax.broadcasted_iota(jnp.int32, (N_PROBLEMS, L), 0) < HEADS
    m = jnp.zeros((N_PROBLEMS, L), F32)
    for c in range(nc):
        cb = nc - 1 - c
        mprev_ref[c, 0:HEADS, :] = m[0:HEADS]
        mprev_ref[cb, HEADS:, :] = m[HEADS:]
        am = jnp.where(fwd_rows, amax[c], amax[cb])
        bl = jnp.where(fwd_rows, b_last[c], b_last[cb])
        m = bl + jnp.maximum(m, am)

    mprev = mprev_ref[...]
    a3 = a.reshape(nc, N_PROBLEMS, L)
    b3 = b.reshape(nc, N_PROBLEMS, L)
    cm3 = cm.reshape(nc, N_PROBLEMS, L)
    sigma = jnp.maximum(mprev, amax)
    mm = jnp.maximum(mprev, cm3)
    out_ref[0, 0] = a3 * LOG2E
    out_ref[0, 1] = mprev * LOG2E
    out_ref[0, 2] = mm * LOG2E
    out_ref[0, 3] = jnp.exp(-(b3 + mm))
    out_ref[0, 4] = jnp.exp(a3 - sigma)
    out_ref[0, 5] = jnp.exp(mprev - sigma)


def _prep(gt, bias_rows):
    bsz, nc, _, L = gt.shape
    return pl.pallas_call(
        _prep_body,
        grid=(bsz,),
        in_specs=[pl.BlockSpec((1, nc, N_GATES, L), lambda b: (b, 0, 0, 0)), _resident(bias_rows.shape)],
        out_specs=pl.BlockSpec((1, N_PREP, nc, N_PROBLEMS, L), lambda b: (b, 0, 0, 0, 0)),
        out_shape=jax.ShapeDtypeStruct((bsz, N_PREP, nc, N_PROBLEMS, L), F32),
        scratch_shapes=[pltpu.VMEM((nc, N_PROBLEMS, L), F32)],
        compiler_params=_params(("parallel",)),
        name="mlstm_gate_prep",
    )(gt, bias_rows)


def _scan_body(qf_ref, ktf_ref, vf_ref, pf_ref, qb_ref, ktb_ref, vb_ref, pb_ref, hf_ref, hb_ref, st_ref):
    L = CHUNK
    cb = pf_ref.shape[2]

    @pl.when(pl.program_id(1) == 0)
    def _():
        st_ref[...] = jnp.zeros(st_ref.shape, F32)

    t_idx = lax.broadcasted_iota(jnp.int32, (L, 2 * L), 0)
    s_idx = lax.broadcasted_iota(jnp.int32, (L, 2 * L), 1)
    masks = ((s_idx <= t_idx) | (s_idx >= L), (s_idx >= t_idx))
    ones_blk = jnp.ones((L, L), BF)

    dirs = ((qf_ref, ktf_ref, vf_ref, pf_ref, hf_ref), (qb_ref, ktb_ref, vb_ref, pb_ref, hb_ref))

    def body(ci, carry):
        probs = []
        for sub in range(CHUNKS_PER_TRIP):
            for d, (q_ref, kt_ref, v_ref, p_ref, h_ref) in enumerate(dirs):
                step = ci * CHUNKS_PER_TRIP + sub
                c = step if d == 0 else cb - 1 - step
                rows = pl.ds(pl.multiple_of(c * L, L), L)
                a2, mprev2, mm2, eneg, w, decay = [p_ref[0, qi, c] for qi in range(N_PREP)]
                mm2_cols = mm2.T
                eneg_cols = eneg.T
                for hd in range(HEADS):
                    rix = d * HEADS + hd
                    qc = q_ref[0, rows, hd * DK:(hd + 1) * DK]
                    ktc = kt_ref[0, hd * DK:(hd + 1) * DK, rows]
                    vc = v_ref[0, rows, hd * DV:(hd + 1) * DV]
                    probs.append(dict(
                        d=d, hd=hd, rows=rows, h_ref=h_ref, qc=qc, ktc=ktc, s=_dot(qc, ktc),
                        top=jnp.concatenate([vc, ones_blk], axis=1),
                        rowvec=jnp.concatenate([a2[rix:rix + 1], mprev2[rix:rix + 1]], axis=1),
                        mm2=mm2_cols[:, rix:rix + 1], eneg=eneg_cols[:, rix:rix + 1],
                        w=w[rix:rix + 1], dec=decay[rix:rix + 1]))
        state = {(d, hd): st_ref[d, hd] for d in range(2) for hd in range(HEADS)}
        for pr in probs:
            st = state[pr["d"], pr["hd"]]
            dmat = jnp.where(masks[pr["d"]], jnp.exp2(pr["rowvec"] - pr["mm2"]), 0.0)
            lhs = (jnp.concatenate([pr["s"], pr["qc"].astype(F32)], axis=1) * dmat).astype(BF)
            pr["main"] = _dot(lhs, jnp.concatenate([pr["top"], st.astype(BF)], axis=0))
            ktw = (pr["ktc"].astype(F32) * pr["w"]).astype(BF)
            dec = pr["dec"]
            state[pr["d"], pr["hd"]] = jnp.concatenate([dec, dec, dec], axis=1) * st + _dot(ktw, pr["top"])
        for pr in probs:
            main = pr["main"]
            r = 1.0 / jnp.maximum(jnp.abs(main[:, DV:]), pr["eneg"])
            h = main[:, 0:DV] * jnp.concatenate([r, r], axis=1)
            pr["h_ref"][0, pr["rows"], pr["hd"] * DV:(pr["hd"] + 1) * DV] = h.astype(BF)
        for (d, hd), st in state.items():
            st_ref[d, hd] = st
        return carry

    lax.fori_loop(0, cb // CHUNKS_PER_TRIP, body, 0)


def _scan(q, kt, v, prep, sb):
    bsz, s, _ = q.shape
    nb = s // sb
    cb = sb // CHUNK
    fwd3 = lambda b, j: (b, j, 0)
    bwd3 = lambda b, j: (b, nb - 1 - j, 0)
    in_specs = []
    for blk3, blkt, blkp in ((fwd3, lambda b, j: (b, 0, j), lambda b, j: (b, 0, j, 0, 0)),
                             (bwd3, lambda b, j: (b, 0, nb - 1 - j), lambda b, j: (b, 0, nb - 1 - j, 0, 0))):
        in_specs += [pl.BlockSpec((1, sb, QK_WIDTH), blk3), pl.BlockSpec((1, QK_WIDTH, sb), blkt),
                     pl.BlockSpec((1, sb, V_WIDTH), blk3), pl.BlockSpec((1, N_PREP, cb, N_PROBLEMS, CHUNK), blkp)]
    return pl.pallas_call(
        _scan_body,
        grid=(bsz, nb),
        in_specs=in_specs,
        out_specs=[pl.BlockSpec((1, sb, V_WIDTH), fwd3), pl.BlockSpec((1, sb, V_WIDTH), bwd3)],
        out_shape=[jax.ShapeDtypeStruct((bsz, s, V_WIDTH), BF)] * 2,
        scratch_shapes=[pltpu.VMEM((2, HEADS, DK, DV + CHUNK), F32)],
        compiler_params=_params(("parallel", "arbitrary"), V7X_SCOPED_VMEM_BYTES),
        name="mlstm_scan",
    )(q, kt, v, prep, q, kt, v, prep)


def _mlp(xn, w1_ref, w2_ref, ff_chunk):
    d_ff = w1_ref.shape[2]
    acc = None
    for c0 in range(0, d_ff, ff_chunk):
        a = _dot(xn, w1_ref[0, :, c0:c0 + ff_chunk])
        a = jnp.square(jnp.maximum(a, 0.0)).astype(BF)
        part = _dot(a, w2_ref[0, c0:c0 + ff_chunk, :])
        acc = part if acc is None else acc + part
    return acc


def _mlp_and_embed(h, p_ref, w1_ref, w2_ref, gw_ref, gb_ref, plew_ref, ff_chunk):
    h = h + _mlp(_rmsnorm(h).astype(BF), w1_ref, w2_ref, ff_chunk)
    gate = jax.nn.sigmoid(_dot(_rmsnorm(h).astype(BF), gw_ref[0]) + gb_ref[...])
    return h + gate * _dot(p_ref[...].astype(BF), plew_ref[0])


def _tail0_body(hf_ref, hb_ref, so_ref, x_ref, p_ref, hnorm_ref, wout_ref, w1_ref, w2_ref,
                gw_ref, gb_ref, plew_ref, pin_ref, h_ref, u_ref, *, ff_chunk):
    hs = hf_ref[...].astype(F32) + hb_ref[...].astype(F32)
    parts = []
    for hd in range(HEADS):
        blk = hs[:, hd * DV:(hd + 1) * DV]
        ms = jnp.mean(blk * blk, axis=-1, keepdims=True)
        parts.append(blk * lax.rsqrt(ms + EPS))
    hn = jnp.concatenate(parts, axis=1) * hnorm_ref[...] * so_ref[...].astype(F32)
    h = x_ref[...] + _dot(hn.astype(BF), wout_ref[...])
    h = _mlp_and_embed(h, p_ref, w1_ref, w2_ref, gw_ref, gb_ref, plew_ref, ff_chunk)
    h_ref[...] = h
    u_ref[...] = _dot(_rmsnorm(h).astype(BF), pin_ref[...])


def _tail0(hf, hb, so, x2, p2, weights, tm, ff_chunk):
    m, d = x2.shape
    return pl.pallas_call(
        functools.partial(_tail0_body, ff_chunk=ff_chunk),
        grid=(m // tm,),
        in_specs=[_rows(tm, d), _rows(tm, d), _rows(tm, d), _rows(tm, d), _rows(tm, p2.shape[1])]
                 + [spec for _, spec in weights],
        out_specs=[_rows(tm, d), _rows(tm, d)],
        out_shape=[jax.ShapeDtypeStruct((m, d), F32), jax.ShapeDtypeStruct((m, d), F32)],
        compiler_params=_params(("parallel",), V7X_SCOPED_VMEM_BYTES),
        name="layer0_tail",
    )(hf, hb, so, x2, p2, *[w for w, _ in weights])


def _layer1_body(u_ref, uprev_ref, unext_ref, h_ref, p_ref, wgrp_ref, scale_ref, pout_ref, w1_ref,
                 w2_ref, gw_ref, gb_ref, plew_ref, nfin_ref, out_ref, ext_ref, lvl_ref, *, seq, ff_chunk):
    tm, d = u_ref.shape
    gw = d // len(POOL_WINDOWS)
    rows = tm + 2 * POOL_HALO
    t0 = (pl.program_id(0) * tm) % seq
    ext_ref[0:POOL_HALO, :] = jnp.where(t0 > 0, uprev_ref[...], 0.0)
    ext_ref[POOL_HALO:POOL_HALO + tm, :] = u_ref[...]
    ext_ref[POOL_HALO + tm:rows, :] = jnp.where(t0 + tm < seq, unext_ref[...], 0.0)
    ext_ref[rows:, :] = jnp.zeros((POOL_HALO, d), F32)
    lvl_ref[:, rows:, :] = jnp.zeros((2, POOL_HALO, gw), F32)

    def window_count(first_row, win):
        t = first_row + lax.broadcasted_iota(jnp.int32, (POOL_HALO, gw), 0)
        return (jnp.minimum(t + (win - win // 2), seq) - jnp.maximum(t - win // 2, 0)).astype(F32)

    mixed = []
    for gi, win in enumerate(POOL_WINDOWS):
        cols = slice(gi * gw, (gi + 1) * gw)
        load = lambda off, n, cols=cols: ext_ref[pl.ds(off, n), cols]
        span, slot = 1, 0
        while 2 * span < win:
            lvl_ref[slot, 0:rows, :] = load(0, rows) + load(span, rows)
            load = lambda off, n, slot=slot: lvl_ref[slot, pl.ds(off, n), :]
            span, slot = 2 * span, 1 - slot
        start = POOL_HALO - win // 2
        total = load(start, tm) + load(start + span, tm)
        mean = jnp.concatenate([total[0:POOL_HALO] / window_count(t0, win),
                                total[POOL_HALO:tm - POOL_HALO] * (1.0 / win),
                                total[tm - POOL_HALO:] / window_count(t0 + tm - POOL_HALO, win)], axis=0)
        mixed.append(_dot((mean - u_ref[:, cols]).astype(BF), wgrp_ref[gi]))
    y = jnp.concatenate(mixed, axis=1) * scale_ref[...]
    h = h_ref[...] + _dot(y.astype(BF), pout_ref[...])
    h = _mlp_and_embed(h, p_ref, w1_ref, w2_ref, gw_ref, gb_ref, plew_ref, ff_chunk)
    out_ref[...] = _rmsnorm(h, nfin_ref[...])


def _layer1(u, h, p2, weights, tm, seq, ff_chunk):
    m, d = u.shape
    n_tiles = m // tm
    hpt = tm // POOL_HALO
    last_halo_block = m // POOL_HALO - 1
    gw = d // len(POOL_WINDOWS)
    return pl.pallas_call(
        functools.partial(_layer1_body, seq=seq, ff_chunk=ff_chunk),
        grid=(n_tiles,),
        in_specs=[_rows(tm, d),
                  pl.BlockSpec((POOL_HALO, d), lambda i: (jnp.maximum(i * hpt - 1, 0), 0)),
                  pl.BlockSpec((POOL_HALO, d), lambda i: (jnp.minimum((i + 1) * hpt, last_halo_block), 0)),
                  _rows(tm, d), _rows(tm, p2.shape[1], first_tile=n_tiles)]
                 + [spec for _, spec in weights],
        out_specs=_rows(tm, d),
        out_shape=jax.ShapeDtypeStruct((m, d), F32),
        scratch_shapes=[pltpu.VMEM((tm + 3 * POOL_HALO, d), F32), pltpu.VMEM((2, tm + 3 * POOL_HALO, gw), F32)],
        compiler_params=_params(("parallel",), V7X_SCOPED_VMEM_BYTES),
        name="layer1_pool_mlp",
    )(u, u, u, h, p2, *[w for w, _ in weights])


def _tile_plan(seq):
    tm_proj, tm_fused, ff_chunk, scan_block = min(1024, seq), min(512, seq), 1024, min(2048, seq)
    assert seq % scan_block == 0 and seq % tm_fused == 0 and seq % tm_proj == 0
    assert scan_block % (CHUNK * CHUNKS_PER_TRIP) == 0 and tm_proj % CHUNK == 0 and tm_fused >= 2 * POOL_HALO
    return tm_proj, tm_fused, ff_chunk, scan_block


def kernel(x, p, norm_mix, norm_mlp, norm_ple, norm_final, mlstm_w_in, mlstm_b_gates, mlstm_head_norm,
           mlstm_w_out, pool_w_in, pool_w_grp, pool_scale, pool_w_out, mlp_w1, mlp_w2, ple_w, ple_gate_w,
           ple_gate_b):
    bsz, seq, d = x.shape
    m = bsz * seq
    tm_proj, tm_fused, ff_chunk, scan_block = _tile_plan(seq)
    row = lambda v: v.reshape(1, -1)
    bf = lambda w: w.astype(BF)

    def whole(v):
        return v, _resident(v.shape)

    def layer(stacked, i):
        return stacked, _layer_of(stacked, i)

    x2 = x.reshape(m, d)
    p2 = p.reshape(-1, p.shape[-1])
    n_main = 2 * QK_WIDTH + 2 * V_WIDTH
    w_in = mlstm_w_in
    w_kt = bf(w_in[0, :, QK_WIDTH:2 * QK_WIDTH].T)
    gate_perm = jnp.array([0, 1, 2, 3, 8, 9, 10, 11, 4, 5, 6, 7, 12, 13, 14, 15], jnp.int32)
    w_gt = bf(w_in[0, :, n_main:].T[gate_perm])
    bias_rows = jnp.broadcast_to(mlstm_b_gates[0].reshape(N_GATES)[gate_perm][:, None], (N_GATES, CHUNK))

    later = [mlp_w1, mlp_w2, ple_gate_w, ple_w, mlstm_w_out, pool_w_in, pool_w_grp, pool_w_out]
    flat = [w.reshape(-1, w.shape[-1]) for w in later]
    steps = m // tm_proj
    assert all(w.shape[0] % (BF16_TILE_ROWS * steps) == 0 for w in flat)
    col = lambda g: g.reshape(-1, 1)
    row_gains = [col(norm_mlp), None, col(norm_ple), None, None, col(norm_mix[1]), None, None]
    (q, kt, v, so, gt), cast = _inproj(x2, row(norm_mix[0]), w_in, w_kt, w_gt, flat, row_gains, tm_proj, seq)
    w1, w2, gate_w, emb_w, w_out, pool_in, pool_grp, pool_out = [c.reshape(w.shape) for c, w in zip(cast, later)]
    prep = _prep(gt, bias_rows)
    hf, hb = _scan(q.reshape(bsz, seq, QK_WIDTH), kt, v.reshape(bsz, seq, V_WIDTH), prep, scan_block)

    def mlp_embed_weights(i):
        return [layer(w1, i), layer(w2, i), layer(gate_w, i), whole(row(ple_gate_b[i])), layer(emb_w, i)]

    tail_weights = [whole(row(mlstm_head_norm[0])), whole(w_out[0])] + mlp_embed_weights(0) + [whole(pool_in[0])]
    h, u = _tail0(hf.reshape(m, V_WIDTH), hb.reshape(m, V_WIDTH), so, x2, p2, tail_weights, tm_fused, ff_chunk)

    l1_weights = ([whole(pool_grp[0]), whole(row(pool_scale[0])), whole(pool_out[0])]
                  + mlp_embed_weights(1) + [whole(row(norm_final))])
    out = _layer1(u, h, p2, l1_weights, tm_fused, seq, ff_chunk)
    return out.reshape(bsz, seq, d)
```
